```python
import jax, jax.numpy as jnp
from jax import lax
import numpy as np

D_MODEL = 2048
BATCH = 2
SEQ = 4096
DEPTH = 4

N_MEM = 256
N_MEM_HEADS = 4
MEM_HEAD_DIM = D_MODEL // N_MEM_HEADS

HEAD_DIM_A = 128
N_HEADS_A = (D_MODEL // 2) // HEAD_DIM_A
WIDTH_A = N_HEADS_A * HEAD_DIM_A
DILATED_BRANCHES = ((128, 1), (512, 4), (2048, 16))

HEAD_DIM_B = 64
N_Q_HEADS_B = (D_MODEL // 2) // HEAD_DIM_B
N_KV_HEADS_B = max(1, N_Q_HEADS_B // 8)
WIDTH_B = N_Q_HEADS_B * HEAD_DIM_B
KV_WIDTH_B = N_KV_HEADS_B * HEAD_DIM_B
WINDOW_B = 128

MIX_WIDTH = WIDTH_A + WIDTH_B
SPLITS = (WIDTH_A, 2 * WIDTH_A, 3 * WIDTH_A, 3 * WIDTH_A + WIDTH_B,
          3 * WIDTH_A + WIDTH_B + KV_WIDTH_B)
IN_WIDTH = 3 * WIDTH_A + WIDTH_B + 2 * KV_WIDTH_B

ROPE_THETA = 500000.0
ROT_DIM_A = HEAD_DIM_A // 4
ROT_DIM_B = HEAD_DIM_B // 4

D_FF = 4 * D_MODEL
BLOCK = 128
ALPHA = (2 * DEPTH) ** 0.25
BETA = (8 * DEPTH) ** -0.25
LN_EPS = 1e-5
RMS_EPS = 1e-6
NEG_INF = -1e30

kernel_name = 'hybrid_dilated_swa_sink_deepnorm'


def _layernorm(x, g, b):
    xf = x.astype(jnp.float32)
    mu = xf.mean(-1, keepdims=True)
    var = jnp.square(xf - mu).mean(-1, keepdims=True)
    return ((xf - mu) * lax.rsqrt(var + LN_EPS) * g + b).astype(x.dtype)


def _rms_gain(y, g):
    yf = y.astype(jnp.float32)
    return yf * lax.rsqrt(jnp.mean(yf * yf, -1, keepdims=True) + RMS_EPS) * g


def _rope_tables(positions, rot_dim):
    inv_freq = ROPE_THETA ** (-jnp.arange(0, rot_dim, 2, dtype=jnp.float32) / rot_dim)
    ang = positions.astype(jnp.float32)[:, :, None] * inv_freq
    return jnp.cos(ang)[:, :, None, :], jnp.sin(ang)[:, :, None, :]


def _partial_rope(t, cos, sin):
    half = cos.shape[-1]
    t1 = t[..., :half].astype(jnp.float32)
    t2 = t[..., half:2 * half].astype(jnp.float32)
    rot = jnp.concatenate([t1 * cos - t2 * sin, t2 * cos + t1 * sin], -1).astype(t.dtype)
    return jnp.concatenate([rot, t[..., 2 * half:]], -1)


def _banded_stats(q, k, v, max_dist, scale):
    B, L, G, R, Dh = q.shape
    nb = L // BLOCK
    qb = q.reshape(B, nb, BLOCK, G, R, Dh).astype(jnp.float32)

    def two_blocks(t):
        tb = t.reshape(B, nb, BLOCK, G, Dh).astype(jnp.float32)
        prev = jnp.pad(tb, ((0, 0), (1, 0), (0, 0), (0, 0), (0, 0)))[:, :-1]
        return jnp.concatenate([prev, tb], axis=2)

    k2, v2 = two_blocks(k), two_blocks(v)
    s = jnp.einsum('bnqgrd,bnkgd->bngrqk', qb, k2) * scale
    q_idx = jnp.arange(BLOCK)[:, None] + BLOCK
    k_idx = jnp.arange(2 * BLOCK)[None, :]
    dist = q_idx - k_idx
    in_band = (dist >= 0) & (dist <= max_dist)
    has_prev = (jnp.arange(nb)[:, None, None] > 0) | (k_idx[None] >= BLOCK)
    valid = in_band[None] & has_prev
    s = jnp.where(valid[None, :, None, None], s, NEG_INF)
    m = s.max(-1)
    p = jnp.exp(s - m[..., None])
    l = p.sum(-1)
    acc = jnp.einsum('bngrqk,bnkgd->bnqgrd', p, v2)
    m = m.transpose(0, 1, 4, 2, 3).reshape(B, L, G, R)
    l = l.transpose(0, 1, 4, 2, 3).reshape(B, L, G, R)
    return m, l, acc.reshape(B, L, G, R, Dh)


def _to_strided(t, d, padded_len):
    B, S, H, Dh = t.shape
    Ld = S // d
    t = t.reshape(B, Ld, d, H, Dh).transpose(0, 2, 1, 3, 4).reshape(B * d, Ld, H, Dh)
    return jnp.pad(t, ((0, 0), (0, padded_len - Ld), (0, 0), (0, 0)))


def _from_strided(t, B, d, Ld):
    t = t[:, :Ld]
    t = t.reshape((B, d, Ld) + t.shape[2:])
    t = jnp.swapaxes(t, 1, 2)
    return t.reshape((B, d * Ld) + t.shape[3:])


def _dilated_attention(q, k, v):
    B, S, H, Dh = q.shape
    ms, ls, accs = [], [], []
    for window, d in DILATED_BRANCHES:
        Ld = S // d
        Lp = -(-Ld // BLOCK) * BLOCK
        m, l, acc = _banded_stats(_to_strided(q, d, Lp)[:, :, :, None, :],
                                  _to_strided(k, d, Lp), _to_strided(v, d, Lp),
                                  window // d, Dh ** -0.5)
        ms.append(_from_strided(m, B, d, Ld))
        ls.append(_from_strided(l, B, d, Ld))
        accs.append(_from_strided(acc, B, d, Ld))
    m = jnp.stack(ms)
    l = jnp.stack(ls)
    acc = jnp.stack(accs)
    c = jnp.exp(m - m.max(0, keepdims=True))
    num = (acc * c[..., None]).sum(0)
    den = (l * c).sum(0)
    return (num / den[..., None]).reshape(B, S, H * Dh)


def _sink_window_gqa(q, k, v, sinks):
    B, S, Hq, Dh = q.shape
    G = k.shape[2]
    R = Hq // G
    m, l, acc = _banded_stats(q.reshape(B, S, G, R, Dh), k, v, WINDOW_B - 1, Dh ** -0.5)
    sink = sinks.astype(jnp.float32).reshape(G, R)
    m2 = jnp.maximum(m, sink)
    c = jnp.exp(m - m2)
    den = l * c + jnp.exp(sink - m2)
    return (acc * (c / den)[..., None]).reshape(B, S, Hq * Dh)


def _hybrid_mixer(x, cos_a, sin_a, cos_b, sin_b, w_in, gn_a, gn_b, sinks, w_out):
    B, S, _ = x.shape
    h = x @ w_in
    qa, ka, va, qb, kb, vb = jnp.split(h, SPLITS, axis=-1)
    qa = _partial_rope(qa.reshape(B, S, N_HEADS_A, HEAD_DIM_A), cos_a, sin_a)
    ka = _partial_rope(ka.reshape(B, S, N_HEADS_A, HEAD_DIM_A), cos_a, sin_a)
    va = va.reshape(B, S, N_HEADS_A, HEAD_DIM_A)
    qb = _partial_rope(qb.reshape(B, S, N_Q_HEADS_B, HEAD_DIM_B), cos_b, sin_b)
    kb = _partial_rope(kb.reshape(B, S, N_KV_HEADS_B, HEAD_DIM_B), cos_b, sin_b)
    vb = vb.reshape(B, S, N_KV_HEADS_B, HEAD_DIM_B)
    ya = _dilated_attention(qa, ka, va)
    yb = _sink_window_gqa(qb, kb, vb, sinks)
    y = jnp.concatenate([_rms_gain(ya, gn_a), _rms_gain(yb, gn_b)], -1).astype(x.dtype)
    return y @ w_out


def _memory_attention(x, mem, w_q, w_kv, w_o):
    B, S, _ = x.shape
    q = (x @ w_q).reshape(B, S, N_MEM_HEADS, MEM_HEAD_DIM).astype(jnp.float32)
    k, v = jnp.split(mem @ w_kv, 2, axis=-1)
    k = k.reshape(B, -1, N_MEM_HEADS, MEM_HEAD_DIM).astype(jnp.float32)
    v = v.reshape(B, -1, N_MEM_HEADS, MEM_HEAD_DIM).astype(jnp.float32)
    p = jax.nn.softmax(jnp.einsum('bshd,bmhd->bhsm', q, k) * MEM_HEAD_DIM ** -0.5, axis=-1)
    o = jnp.einsum('bhsm,bmhd->bshd', p, v).reshape(B, S, D_MODEL).astype(x.dtype)
    return o @ w_o


def _sq_relu_mlp(x, w_up, w_down):
    h = jax.nn.relu(x @ w_up)
    return (h * h) @ w_down


def setup_inputs(seed: int = 0) -> dict:
    key = jax.random.key(seed)
    ks = jax.random.split(key, 20)
    f32 = jnp.float32

    def nrm(k, shape, scale):
        return jax.random.normal(k, shape, f32) * scale

    def gain(k, shape):
        return 1.0 + 0.02 * jax.random.normal(k, shape, f32)

    x = nrm(ks[0], (BATCH, SEQ, D_MODEL), 1.0)
    mem = nrm(ks[1], (BATCH, N_MEM, D_MODEL), 1.0)
    positions = (jnp.arange(SEQ, dtype=jnp.int32)[None, :]
                 + jax.random.randint(ks[2], (BATCH, 1), 0, 4096, dtype=jnp.int32))
    in_col_scale = jnp.concatenate([
        jnp.ones((2 * WIDTH_A,), f32), jnp.full((WIDTH_A,), BETA, f32),
        jnp.ones((WIDTH_B + KV_WIDTH_B,), f32), jnp.full((KV_WIDTH_B,), BETA, f32)])
    w_in = nrm(ks[3], (DEPTH, D_MODEL, IN_WIDTH), D_MODEL ** -0.5) * in_col_scale
    gn_a = gain(ks[4], (DEPTH, WIDTH_A))
    gn_b = gain(ks[5], (DEPTH, WIDTH_B))
    sinks = nrm(ks[6], (DEPTH, N_Q_HEADS_B), 0.5)
    w_out = nrm(ks[7], (DEPTH, MIX_WIDTH, D_MODEL), BETA * MIX_WIDTH ** -0.5)
    ln_mix_g = gain(ks[8], (DEPTH, D_MODEL))
    ln_mix_b = nrm(ks[9], (DEPTH, D_MODEL), 0.02)
    w_mq = nrm(ks[10], (DEPTH, D_MODEL, D_MODEL), D_MODEL ** -0.5)
    kv_col_scale = jnp.concatenate([jnp.ones((D_MODEL,), f32), jnp.full((D_MODEL,), BETA, f32)])
    w_mkv = nrm(ks[11], (DEPTH, D_MODEL, 2 * D_MODEL), D_MODEL ** -0.5) * kv_col_scale
    w_mo = nrm(ks[12], (DEPTH, D_MODEL, D_MODEL), BETA * D_MODEL ** -0.5)
    ln_mem_g = gain(ks[13], (DEPTH, D_MODEL))
    ln_mem_b = nrm(ks[14], (DEPTH, D_MODEL), 0.02)
    w_up = nrm(ks[15], (DEPTH, D_MODEL, D_FF), D_MODEL ** -0.5)
    w_down = nrm(ks[16], (DEPTH, D_FF, D_MODEL), BETA * D_FF ** -0.5)
    ln_ff_g = gain(ks[17], (DEPTH, D_MODEL))
    ln_ff_b = nrm(ks[18], (DEPTH, D_MODEL), 0.02)
    return {'x': x, 'mem': mem, 'positions': positions, 'w_in': w_in, 'gn_a': gn_a,
            'gn_b': gn_b, 'sinks': sinks, 'w_out': w_out, 'ln_mix_g': ln_mix_g,
            'ln_mix_b': ln_mix_b, 'w_mq': w_mq, 'w_mkv': w_mkv, 'w_mo': w_mo,
            'ln_mem_g': ln_mem_g, 'ln_mem_b': ln_mem_b, 'w_up': w_up, 'w_down': w_down,
            'ln_ff_g': ln_ff_g, 'ln_ff_b': ln_ff_b}


def reference(x, mem, positions, w_in, gn_a, gn_b, sinks, w_out, ln_mix_g, ln_mix_b,
              w_mq, w_mkv, w_mo, ln_mem_g, ln_mem_b, w_up, w_down, ln_ff_g, ln_ff_b):
    cos_a, sin_a = _rope_tables(positions, ROT_DIM_A)
    cos_b, sin_b = _rope_tables(positions, ROT_DIM_B)
    for i in range(DEPTH):
        y = _hybrid_mixer(x, cos_a, sin_a, cos_b, sin_b, w_in[i], gn_a[i], gn_b[i],
                          sinks[i], w_out[i])
        x = _layernorm(ALPHA * x + y, ln_mix_g[i], ln_mix_b[i])
        y = _memory_attention(x, mem, w_mq[i], w_mkv[i], w_mo[i])
        x = _layernorm(ALPHA * x + y, ln_mem_g[i], ln_mem_b[i])
        y = _sq_relu_mlp(x, w_up[i], w_down[i])
        x = _layernorm(ALPHA * x + y, ln_ff_g[i], ln_ff_b[i])
    return x
```

```python
import functools

import jax
import jax.numpy as jnp
from jax import lax
from jax.experimental import pallas as pl
from jax.experimental.pallas import tpu as pltpu

D_MODEL = 2048
DEPTH = 4
N_MEM_HEADS = 4
MEM_HEAD_DIM = D_MODEL // N_MEM_HEADS
HEAD_DIM_A = 128
N_HEADS_A = 8
WIDTH_A = N_HEADS_A * HEAD_DIM_A
DILATIONS = (1, 4, 16)
HEAD_DIM_B = 64
N_Q_HEADS_B = 16
N_KV_HEADS_B = 2
WIDTH_B = N_Q_HEADS_B * HEAD_DIM_B
KV_WIDTH_B = N_KV_HEADS_B * HEAD_DIM_B
ROPE_THETA = 500000.0
ROT_DIM_A = HEAD_DIM_A // 4
ROT_DIM_B = HEAD_DIM_B // 4
D_FF = 4 * D_MODEL
BLOCK = 128
ALPHA = (2 * DEPTH) ** 0.25
LN_EPS = 1e-5
RMS_EPS = 1e-6
NEG_INF = -1e30

BF16 = jnp.bfloat16
F32 = jnp.float32

VMEM_LIMIT_BYTES = 56 * 1024 * 1024


def _params(*semantics):
    return pltpu.CompilerParams(dimension_semantics=semantics,
                                vmem_limit_bytes=VMEM_LIMIT_BYTES)


def _resident(block_shape, index_map):
    return pl.BlockSpec(block_shape, index_map, pipeline_mode=pl.Buffered(1))


def _layernorm(z, g, b):
    mu = jnp.mean(z, axis=-1, keepdims=True)
    zc = z - mu
    var = jnp.mean(zc * zc, axis=-1, keepdims=True)
    return zc * lax.rsqrt(var + LN_EPS) * g + b


def _rope(t, c, s_up, s_dn, half):
    return (t * c + pltpu.roll(t, half, 1) * s_up
            + pltpu.roll(t, BLOCK - half, 1) * s_dn)


def _inproj_kernel(x_ref, w_ref, ca_ref, ua_ref, da_ref, cb_ref, ub_ref, db_ref,
                   qa_ref, ka_ref, va_ref, qb_ref, kvb_ref):
    x = x_ref[...].astype(BF16)
    ca, ua, da = ca_ref[...], ua_ref[...], da_ref[...]
    cb, ub, db = cb_ref[...], ub_ref[...], db_ref[...]
    scale_a = HEAD_DIM_A ** -0.5
    scale_b = HEAD_DIM_B ** -0.5
    chunk = 4 * BLOCK

    def proj(col, width):
        return jnp.dot(x, w_ref[:, col:col + width], preferred_element_type=F32)

    for c0 in range(0, WIDTH_A, chunk):
        hq = proj(c0, chunk)
        hk = proj(WIDTH_A + c0, chunk)
        for j in range(0, chunk, BLOCK):
            sl = slice(c0 + j, c0 + j + BLOCK)
            qa_ref[:, sl] = (_rope(hq[:, j:j + BLOCK], ca, ua, da, ROT_DIM_A // 2)
                             * scale_a).astype(BF16)
            ka_ref[:, sl] = _rope(hk[:, j:j + BLOCK], ca, ua, da,
                                  ROT_DIM_A // 2).astype(BF16)
        va_ref[:, c0:c0 + chunk] = proj(2 * WIDTH_A + c0, chunk).astype(BF16)
    for c0 in range(0, WIDTH_B, chunk):
        hq = proj(3 * WIDTH_A + c0, chunk)
        for j in range(0, chunk, BLOCK):
            qb_ref[:, c0 + j:c0 + j + BLOCK] = (
                _rope(hq[:, j:j + BLOCK], cb, ub, db, ROT_DIM_B // 2) * scale_b).astype(BF16)
    hkv = proj(3 * WIDTH_A + WIDTH_B, 2 * KV_WIDTH_B)
    kvb_ref[:, :KV_WIDTH_B] = _rope(hkv[:, :KV_WIDTH_B], cb, ub, db,
                                    ROT_DIM_B // 2).astype(BF16)
    kvb_ref[:, KV_WIDTH_B:] = hkv[:, KV_WIDTH_B:].astype(BF16)


def _inproj(x, w, tabs_a, tabs_b, tm=512):
    m = x.shape[0]
    in_width = w.shape[1]
    row = lambda i: (i, 0)
    tab_spec = pl.BlockSpec((tm, BLOCK), row)
    out = lambda width: jax.ShapeDtypeStruct((m, width), BF16)
    return pl.pallas_call(
        _inproj_kernel,
        grid=(m // tm,),
        in_specs=[pl.BlockSpec((tm, D_MODEL), row),
                  _resident((D_MODEL, in_width), lambda i: (0, 0))] + [tab_spec] * 6,
        out_specs=[pl.BlockSpec((tm, WIDTH_A), row)] * 3
        + [pl.BlockSpec((tm, WIDTH_B), row), pl.BlockSpec((tm, 2 * KV_WIDTH_B), row)],
        out_shape=[out(WIDTH_A)] * 3 + [out(WIDTH_B), out(2 * KV_WIDTH_B)],
        compiler_params=_params("arbitrary"),
        name="inproj",
    )(x, w, *tabs_a, *tabs_b)


def _band_masks(max_dist, has_prev):
    qi = lax.broadcasted_iota(jnp.int32, (BLOCK, 2 * BLOCK), 0)
    kj = lax.broadcasted_iota(jnp.int32, (BLOCK, 2 * BLOCK), 1)
    dist = qi + BLOCK - kj
    in_band = (dist >= 0) & (dist <= max_dist)
    first_key = jnp.where(has_prev, 0, BLOCK)
    return in_band, in_band & (kj >= first_key)


def _attn_a_kernel(q_ref, kp_ref, k_ref, vp_ref, v_ref, o_ref, lse_ref, *, rows, groups):
    in_band, first_mask = _band_masks(BLOCK, pl.program_id(1) > 0)
    lane = lax.broadcasted_iota(jnp.int32, (BLOCK, BLOCK), 1)
    for g in range(groups):
        for j in range(rows // BLOCK):
            r0 = j * BLOCK
            lse_tile = jnp.zeros((BLOCK, BLOCK), F32)
            for h in range(N_HEADS_A):
                cols = slice(g * WIDTH_A + h * HEAD_DIM_A, g * WIDTH_A + (h + 1) * HEAD_DIM_A)
                q = q_ref[r0:r0 + BLOCK, cols]
                if j == 0:
                    k2 = jnp.concatenate([kp_ref[:, cols], k_ref[:BLOCK, cols]], axis=0)
                    v2 = jnp.concatenate([vp_ref[:, cols], v_ref[:BLOCK, cols]], axis=0)
                    mask = first_mask
                else:
                    k2 = k_ref[r0 - BLOCK:r0 + BLOCK, cols]
                    v2 = v_ref[r0 - BLOCK:r0 + BLOCK, cols]
                    mask = in_band
                s = lax.dot_general(q, k2, (((1,), (1,)), ((), ())),
                                    preferred_element_type=F32)
                s = jnp.where(mask, s, NEG_INF)
                m = jnp.max(s, axis=-1, keepdims=True)
                p = jnp.exp(s - m)
                l = jnp.sum(p, axis=-1, keepdims=True)
                acc = jnp.dot(p.astype(BF16), v2, preferred_element_type=F32)
                o_ref[r0:r0 + BLOCK, cols] = (acc / l).astype(BF16)
                lse_tile = jnp.where(lane == h, m + jnp.log(l), lse_tile)
            lse_ref[r0:r0 + BLOCK, g * BLOCK:(g + 1) * BLOCK] = lse_tile


def _attn_a(qa, ka, va, batch, seq, d):
    ld = seq // d
    rows = min(ld, 512)
    groups = max(1, 512 // rows) if d > 1 else 1
    groups = min(groups, d)
    nblk = ld // rows
    prev_per_blk = rows // BLOCK
    view = lambda t, width: t.reshape(batch, ld, d * width)
    cur = lambda b, n, r: (b, n, r)
    prev = lambda b, n, r: (b, jnp.maximum(n * prev_per_blk - 1, 0), r)
    big = pl.BlockSpec((None, rows, groups * WIDTH_A), cur)
    small = pl.BlockSpec((None, BLOCK, groups * WIDTH_A), prev)
    o, lse = pl.pallas_call(
        functools.partial(_attn_a_kernel, rows=rows, groups=groups),
        grid=(batch, nblk, d // groups),
        in_specs=[big, small, big, small, big],
        out_specs=[big, pl.BlockSpec((None, rows, groups * BLOCK), cur)],
        out_shape=[jax.ShapeDtypeStruct((batch, ld, d * WIDTH_A), BF16),
                   jax.ShapeDtypeStruct((batch, ld, d * BLOCK), F32)],
        compiler_params=_params("arbitrary", "arbitrary", "arbitrary"),
        name=f"attn_a_d{d}",
    )(view(qa, WIDTH_A), view(ka, WIDTH_A), view(ka, WIDTH_A), view(va, WIDTH_A),
      view(va, WIDTH_A))
    return o.reshape(batch * seq, WIDTH_A), lse.reshape(batch * seq, BLOCK)


def _attn_b_kernel(sink_ref, q_ref, kvp_ref, kv_ref, o_ref, *, rows):
    in_band, first_mask = _band_masks(BLOCK - 1, pl.program_id(1) > 0)
    stack = lambda t: jnp.concatenate([t, t], axis=0)
    lane = lax.broadcasted_iota(jnp.int32, (BLOCK, BLOCK), 1)
    low = lane < HEAD_DIM_B
    row2 = lax.broadcasted_iota(jnp.int32, (2 * BLOCK, 1), 0)
    heads_per_group = N_Q_HEADS_B // N_KV_HEADS_B
    for j in range(rows // BLOCK):
        r0 = j * BLOCK
        if j == 0:
            kv2 = jnp.concatenate([kvp_ref[...], kv_ref[:BLOCK, :]], axis=0)
            mask = stack(first_mask)
        else:
            kv2 = kv_ref[r0 - BLOCK:r0 + BLOCK, :]
            mask = stack(in_band)
        kv2 = kv2.astype(F32)
        k2, v2 = kv2[:, :KV_WIDTH_B], kv2[:, KV_WIDTH_B:]
        low2 = lax.broadcasted_iota(jnp.int32, (2 * BLOCK, BLOCK), 1) < HEAD_DIM_B
        k2r, v2r = pltpu.roll(k2, HEAD_DIM_B, 1), pltpu.roll(v2, HEAD_DIM_B, 1)
        kdup = [jnp.where(low2, k2, k2r).astype(BF16), jnp.where(low2, k2r, k2).astype(BF16)]
        vdup = [jnp.where(low2, v2, v2r).astype(BF16), jnp.where(low2, v2r, v2).astype(BF16)]
        for t in range(WIDTH_B // BLOCK):
            g = (2 * t) // heads_per_group
            cols = slice(t * BLOCK, (t + 1) * BLOCK)
            q = q_ref[r0:r0 + BLOCK, cols]
            zero = jnp.zeros_like(q)
            q2 = jnp.concatenate([jnp.where(low, q, zero), jnp.where(low, zero, q)], axis=0)
            s = lax.dot_general(q2, kdup[g], (((1,), (1,)), ((), ())),
                                preferred_element_type=F32)
            s = jnp.where(mask, s, NEG_INF)
            m = jnp.max(s, axis=-1, keepdims=True)
            p = jnp.exp(s - m)
            l = jnp.sum(p, axis=-1, keepdims=True)
            acc = jnp.dot(p.astype(BF16), vdup[g], preferred_element_type=F32)
            sink = jnp.where(row2 < BLOCK, sink_ref[2 * t], sink_ref[2 * t + 1])
            m2 = jnp.maximum(m, sink)
            c = jnp.exp(m - m2)
            den = l * c + jnp.exp(sink - m2)
            y = acc * (c / den)
            o_ref[r0:r0 + BLOCK, cols] = jnp.where(low, y[:BLOCK], y[BLOCK:]).astype(BF16)


def _attn_b(sinks, qb, kvb, batch, seq, rows=512):
    nblk = seq // rows
    prev_per_blk = rows // BLOCK
    view = lambda t: t.reshape(batch, seq, t.shape[-1])
    cur = lambda b, n: (b, n, 0)
    prev = lambda b, n: (b, jnp.maximum(n * prev_per_blk - 1, 0), 0)
    o = pl.pallas_call(
        functools.partial(_attn_b_kernel, rows=rows),
        grid=(batch, nblk),
        in_specs=[pl.BlockSpec(memory_space=pltpu.SMEM),
                  pl.BlockSpec((None, rows, WIDTH_B), cur),
                  pl.BlockSpec((None, BLOCK, 2 * KV_WIDTH_B), prev),
                  pl.BlockSpec((None, rows, 2 * KV_WIDTH_B), cur)],
        out_specs=pl.BlockSpec((None, rows, WIDTH_B), cur),
        out_shape=jax.ShapeDtypeStruct((batch, seq, WIDTH_B), BF16),
        compiler_params=_params("arbitrary", "arbitrary"),
        name="attn_b",
    )(sinks, view(qb), view(kvb), view(kvb))
    return o.reshape(batch * seq, WIDTH_B)


def _outproj_kernel(o1_ref, o4_ref, o16_ref, l1_ref, l4_ref, l16_ref, yb_ref, x_ref, w_ref,
                    gna_ref, gnb_ref, g_ref, b_ref, out_ref):
    lses = [l1_ref[...], l4_ref[...], l16_ref[...]]
    top = jnp.maximum(jnp.maximum(lses[0], lses[1]), lses[2])
    ws = [jnp.exp(l - top) for l in lses]
    inv = 1.0 / (ws[0] + ws[1] + ws[2])
    ws = [w * inv for w in ws]
    o_refs = [o1_ref, o4_ref, o16_ref]
    tiles = []
    ssq = jnp.zeros((x_ref.shape[0], 1), F32)
    for h in range(N_HEADS_A):
        cols = slice(h * HEAD_DIM_A, (h + 1) * HEAD_DIM_A)
        t = sum(ws[i][:, h:h + 1] * o_refs[i][:, cols].astype(F32) for i in range(3))
        ssq = ssq + jnp.sum(t * t, axis=-1, keepdims=True)
        tiles.append(t)
    rms_a = lax.rsqrt(ssq / WIDTH_A + RMS_EPS)
    ya = jnp.concatenate(tiles, axis=1) * rms_a * gna_ref[...]
    yb = yb_ref[...].astype(F32)
    rms_b = lax.rsqrt(jnp.mean(yb * yb, axis=-1, keepdims=True) + RMS_EPS)
    yb = yb * rms_b * gnb_ref[...]
    y = jnp.dot(ya.astype(BF16), w_ref[:WIDTH_A, :], preferred_element_type=F32)
    y = y + jnp.dot(yb.astype(BF16), w_ref[WIDTH_A:, :], preferred_element_type=F32)
    out_ref[...] = _layernorm(ALPHA * x_ref[...] + y, g_ref[...], b_ref[...])


def _outproj(os, lses, yb, x, w, gna, gnb, g, b, tm=512):
    m = x.shape[0]
    row = lambda i: (i, 0)
    fixed = lambda i: (0, 0)
    vec = lambda width: pl.BlockSpec((1, width), fixed)
    return pl.pallas_call(
        _outproj_kernel,
        grid=(m // tm,),
        in_specs=[pl.BlockSpec((tm, WIDTH_A), row)] * 3 + [pl.BlockSpec((tm, BLOCK), row)] * 3
        + [pl.BlockSpec((tm, WIDTH_B), row), pl.BlockSpec((tm, D_MODEL), row),
           _resident((WIDTH_A + WIDTH_B, D_MODEL), fixed),
           vec(WIDTH_A), vec(WIDTH_B), vec(D_MODEL), vec(D_MODEL)],
        out_specs=pl.BlockSpec((tm, D_MODEL), row),
        out_shape=jax.ShapeDtypeStruct((m, D_MODEL), F32),
        compiler_params=_params("arbitrary"),
        name="outproj",
    )(*os, *lses, yb, x, w, gna, gnb, g, b)


def _memkv_kernel(mem_ref, w_ref, kv_ref):
    kv_ref[...] = jnp.dot(mem_ref[...].astype(BF16), w_ref[...].astype(BF16),
                          preferred_element_type=F32).astype(BF16)


def _memkv(mem, w_mkv, tn=1024):
    rows = mem.shape[0]
    return pl.pallas_call(
        _memkv_kernel,
        grid=(DEPTH, 2 * D_MODEL // tn),
        in_specs=[pl.BlockSpec((rows, D_MODEL), lambda i, j: (0, 0)),
                  pl.BlockSpec((None, D_MODEL, tn), lambda i, j: (i, 0, j))],
        out_specs=pl.BlockSpec((None, rows, tn), lambda i, j: (i, 0, j)),
        out_shape=jax.ShapeDtypeStruct((DEPTH, rows, 2 * D_MODEL), BF16),
        compiler_params=_params("arbitrary", "arbitrary"),
        name="memkv",
    )(mem, w_mkv)


def _memattn_kernel(x_ref, wq_ref, k_ref, v_ref, wo_ref, g_ref, b_ref, out_ref):
    x = x_ref[...]
    q = jnp.dot(x.astype(BF16), wq_ref[...], preferred_element_type=F32)
    q = (q * MEM_HEAD_DIM ** -0.5).astype(BF16)
    heads = []
    for h in range(N_MEM_HEADS):
        cols = slice(h * MEM_HEAD_DIM, (h + 1) * MEM_HEAD_DIM)
        s = lax.dot_general(q[:, cols], k_ref[:, cols], (((1,), (1,)), ((), ())),
                            preferred_element_type=F32)
        m = jnp.max(s, axis=-1, keepdims=True)
        p = jnp.exp(s - m)
        l = jnp.sum(p, axis=-1, keepdims=True)
        o = jnp.dot(p.astype(BF16), v_ref[:, cols], preferred_element_type=F32)
        heads.append((o / l).astype(BF16))
    y = jnp.dot(jnp.concatenate(heads, axis=1), wo_ref[...], preferred_element_type=F32)
    out_ref[...] = _layernorm(ALPHA * x + y, g_ref[...], b_ref[...])


def _memattn(x, wq, kv, wo, g, b, batch, seq, layer, tm=512):
    n_mem = kv.shape[1] // batch
    per_batch = seq // tm
    row = lambda i: (i, 0)
    fixed = lambda i: (0, 0)
    vec = pl.BlockSpec((1, D_MODEL), fixed)
    kv_spec = lambda half: pl.BlockSpec((None, n_mem, D_MODEL),
                                        lambda i: (layer, i // per_batch, half))
    return pl.pallas_call(
        _memattn_kernel,
        grid=(batch * seq // tm,),
        in_specs=[pl.BlockSpec((tm, D_MODEL), row), _resident((D_MODEL, D_MODEL), fixed),
                  kv_spec(0), kv_spec(1), _resident((D_MODEL, D_MODEL), fixed), vec, vec],
        out_specs=pl.BlockSpec((tm, D_MODEL), row),
        out_shape=jax.ShapeDtypeStruct((batch * seq, D_MODEL), F32),
        compiler_params=_params("arbitrary"),
        name="memattn",
    )(x, wq, kv, kv, wo, g, b)


def _mlp_kernel(x_ref, wu_ref, wd_ref, g_ref, b_ref, out_ref, xb_ref, acc_ref):
    j = pl.program_id(1)

    @pl.when(j == 0)
    def _():
        xb_ref[...] = x_ref[...].astype(BF16)

    h = jnp.maximum(jnp.dot(xb_ref[...], wu_ref[...], preferred_element_type=F32), 0.0)
    part = jnp.dot((h * h).astype(BF16), wd_ref[...], preferred_element_type=F32)

    @pl.when(j == 0)
    def _():
        acc_ref[...] = part

    @pl.when(j > 0)
    def _():
        acc_ref[...] += part

    @pl.when(j == pl.num_programs(1) - 1)
    def _():
        out_ref[...] = _layernorm(ALPHA * x_ref[...] + acc_ref[...], g_ref[...], b_ref[...])


def _mlp(x, wu, wd, g, b, tm=512, tf=1024):
    m = x.shape[0]
    row = lambda i, j: (i, 0)
    vec = pl.BlockSpec((1, D_MODEL), lambda i, j: (0, 0))
    return pl.pallas_call(
        _mlp_kernel,
        grid=(m // tm, D_FF // tf),
        in_specs=[pl.BlockSpec((tm, D_MODEL), row),
                  pl.BlockSpec((D_MODEL, tf), lambda i, j: (0, j)),
                  pl.BlockSpec((tf, D_MODEL), lambda i, j: (j, 0)), vec, vec],
        out_specs=pl.BlockSpec((tm, D_MODEL), row),
        out_shape=jax.ShapeDtypeStruct((m, D_MODEL), F32),
        scratch_shapes=[pltpu.VMEM((tm, D_MODEL), BF16), pltpu.VMEM((tm, D_MODEL), F32)],
        compiler_params=_params("arbitrary", "arbitrary"),
        name="mlp",
    )(x, wu, wd, g, b)


def _rope_lane_tables(positions, head_dim, rot_dim):
    half = rot_dim // 2
    inv_freq = ROPE_THETA ** (-jnp.arange(0, rot_dim, 2, dtype=F32) / rot_dim)
    ang = positions.astype(F32).reshape(-1, 1) * inv_freq
    cos, sin = jnp.cos(ang), jnp.sin(ang)
    rows = ang.shape[0]
    pad = jnp.zeros((rows, head_dim - rot_dim), F32)
    zero = jnp.zeros((rows, half), F32)
    reps = BLOCK // head_dim
    c = jnp.tile(jnp.concatenate([cos, cos, pad + 1.0], axis=1), (1, reps))
    s_up = jnp.tile(jnp.concatenate([zero, sin, pad], axis=1), (1, reps))
    s_dn = jnp.tile(jnp.concatenate([-sin, zero, pad], axis=1), (1, reps))
    return c, s_up, s_dn


def kernel(x, mem, positions, w_in, gn_a, gn_b, sinks, w_out, ln_mix_g, ln_mix_b, w_mq, w_mkv,
           w_mo, ln_mem_g, ln_mem_b, w_up, w_down, ln_ff_g, ln_ff_b):
    batch, seq, _ = x.shape
    tabs_a = _rope_lane_tables(positions, HEAD_DIM_A, ROT_DIM_A)
    tabs_b = _rope_lane_tables(positions, HEAD_DIM_B, ROT_DIM_B)
    w_in, w_out, w_mq, w_mo, w_up, w_down = (
        w.astype(BF16) for w in (w_in, w_out, w_mq, w_mo, w_up, w_down))
    kv = _memkv(mem.reshape(-1, D_MODEL), w_mkv)
    vec = lambda v: v.reshape(1, -1)
    h = x.reshape(batch * seq, D_MODEL)
    for i in range(DEPTH):
        qa, ka, va, qb, kvb = _inproj(h, w_in[i], tabs_a, tabs_b)
        branches = [_attn_a(qa, ka, va, batch, seq, d) for d in DILATIONS]
        yb = _attn_b(sinks[i], qb, kvb, batch, seq)
        h = _outproj([o for o, _ in branches], [l for _, l in branches], yb, h, w_out[i],
                     vec(gn_a[i]), vec(gn_b[i]), vec(ln_mix_g[i]), vec(ln_mix_b[i]))
        h = _memattn(h, w_mq[i], kv, w_mo[i], vec(ln_mem_g[i]), vec(ln_mem_b[i]),
                     batch, seq, i)
        h = _mlp(h, w_up[i], w_down[i], vec(ln_ff_g[i]), vec(ln_ff_b[i]))
    return h.reshape(batch, seq, D_MODEL)
```

```python
import functools

import jax
import jax.numpy as jnp
from jax import lax
from jax.experimental import pallas as pl
from jax.experimental.pallas import tpu as pltpu

D_MODEL = 2048
DEPTH = 4
N_MEM_HEADS = 4
MEM_HEAD_DIM = D_MODEL // N_MEM_HEADS
HEAD_DIM_A = 128
N_HEADS_A = 8
WIDTH_A = N_HEADS_A * HEAD_DIM_A
HEAD_DIM_B = 64
N_Q_HEADS_B = 16
N_KV_HEADS_B = 2
WIDTH_B = N_Q_HEADS_B * HEAD_DIM_B
KV_WIDTH_B = N_KV_HEADS_B * HEAD_DIM_B
ROPE_THETA = 500000.0
ROT_DIM_A = HEAD_DIM_A // 4
ROT_DIM_B = HEAD_DIM_B // 4
D_FF = 4 * D_MODEL
BLOCK = 128
ALPHA = (2 * DEPTH) ** 0.25
LN_EPS = 1e-5
RMS_EPS = 1e-6
NEG_INF = -1e30

PIECE = 16
GROUP = PIECE * PIECE
UNROLL = 4

BF16 = jnp.bfloat16
F32 = jnp.float32

VMEM_LIMIT_BYTES = 56 * 1024 * 1024


def _params(*semantics):
    return pltpu.CompilerParams(dimension_semantics=semantics,
                                vmem_limit_bytes=VMEM_LIMIT_BYTES)


def _single_buffered(block_shape, index_map):
    return pl.BlockSpec(block_shape, index_map, pipeline_mode=pl.Buffered(1))


def _layernorm(z, g, b):
    mu = jnp.mean(z, axis=-1, keepdims=True)
    zc = z - mu
    var = jnp.mean(zc * zc, axis=-1, keepdims=True)
    return zc * lax.rsqrt(var + LN_EPS) * g + b


def _dot_nt(a, b):
    return lax.dot_general(a, b, (((1,), (1,)), ((), ())), preferred_element_type=F32)


def _dot(a, b):
    return jnp.dot(a, b, preferred_element_type=F32)


def _rope(t, c, s_up, s_dn, half):
    return (t * c + pltpu.roll(t, half, 1) * s_up
            + pltpu.roll(t, BLOCK - half, 1) * s_dn)


def _inproj_kernel(x_ref, w_ref, ca_ref, ua_ref, da_ref, cb_ref, ub_ref, db_ref,
                   qa_ref, ka_ref, va_ref, qb_ref, kvb_ref):
    x = x_ref[...].astype(BF16)
    ca, ua, da = ca_ref[...], ua_ref[...], da_ref[...]
    cb, ub, db = cb_ref[...], ub_ref[...], db_ref[...]
    scale_a = HEAD_DIM_A ** -0.5
    scale_b = HEAD_DIM_B ** -0.5
    chunk = 4 * BLOCK

    def proj(col, width):
        return _dot(x, w_ref[:, col:col + width])

    for c0 in range(0, WIDTH_A, chunk):
        hq = proj(c0, chunk)
        hk = proj(WIDTH_A + c0, chunk)
        for j in range(0, chunk, BLOCK):
            sl = slice(c0 + j, c0 + j + BLOCK)
            qa_ref[:, sl] = (_rope(hq[:, j:j + BLOCK], ca, ua, da, ROT_DIM_A // 2)
                             * scale_a).astype(BF16)
            ka_ref[:, sl] = _rope(hk[:, j:j + BLOCK], ca, ua, da,
                                  ROT_DIM_A // 2).astype(BF16)
        va_ref[:, c0:c0 + chunk] = proj(2 * WIDTH_A + c0, chunk).astype(BF16)
    for c0 in range(0, WIDTH_B, chunk):
        hq = proj(3 * WIDTH_A + c0, chunk)
        for j in range(0, chunk, BLOCK):
            qb_ref[:, c0 + j:c0 + j + BLOCK] = (
                _rope(hq[:, j:j + BLOCK], cb, ub, db, ROT_DIM_B // 2) * scale_b).astype(BF16)
    hkv = proj(3 * WIDTH_A + WIDTH_B, 2 * KV_WIDTH_B)
    kvb_ref[:, :KV_WIDTH_B] = _rope(hkv[:, :KV_WIDTH_B], cb, ub, db,
                                    ROT_DIM_B // 2).astype(BF16)
    kvb_ref[:, KV_WIDTH_B:] = hkv[:, KV_WIDTH_B:].astype(BF16)


def _inproj(x, w, layer, tabs_a, tabs_b, tm=512):
    m = x.shape[0]
    in_width = w.shape[-1]
    row = lambda i: (i, 0)
    tab_spec = pl.BlockSpec((tm, BLOCK), row)
    out = lambda width: jax.ShapeDtypeStruct((m, width), BF16)
    return pl.pallas_call(
        _inproj_kernel,
        grid=(m // tm,),
        in_specs=[pl.BlockSpec((tm, D_MODEL), row),
                  _single_buffered((None, D_MODEL, in_width), lambda i: (layer, 0, 0))]
        + [tab_spec] * 6,
        out_specs=[pl.BlockSpec((tm, WIDTH_A), row)] * 3
        + [pl.BlockSpec((tm, WIDTH_B), row), pl.BlockSpec((tm, 2 * KV_WIDTH_B), row)],
        out_shape=[out(WIDTH_A)] * 3 + [out(WIDTH_B), out(2 * KV_WIDTH_B)],
        compiler_params=_params("arbitrary"),
        name="inproj",
    )(x, w, *tabs_a, *tabs_b)


def _band_mask(max_dist, n_keys=2 * BLOCK):
    qi = lax.broadcasted_iota(jnp.int32, (BLOCK, n_keys), 0) + (n_keys - BLOCK)
    kj = lax.broadcasted_iota(jnp.int32, (BLOCK, n_keys), 1)
    dist = qi - kj
    return (dist >= 0) & (dist <= max_dist)


def _drop_missing_prev(mask, is_first):
    kj = lax.broadcasted_iota(jnp.int32, mask.shape, 1)
    return mask & (kj >= jnp.where(is_first, BLOCK, 0))


def _softmax_block(q, k, v, mask):
    s = jnp.where(mask, _dot_nt(q, k), NEG_INF)
    m = jnp.max(s, axis=-1, keepdims=True)
    p = jnp.exp(s - m)
    l = jnp.sum(p, axis=-1, keepdims=True)
    return m, l, _dot(p.astype(BF16), v)


def _merge(o_prev, lse_prev, m, l, acc):
    top = jnp.maximum(lse_prev, m)
    w_prev = jnp.exp(lse_prev - top)
    w_new = jnp.exp(m - top)
    den = w_prev + l * w_new
    return (o_prev * w_prev + acc * w_new) * (1.0 / den), top + jnp.log(den)


def _gather(ref, offsets, cols):
    return jnp.concatenate(
        [ref[pl.ds(pl.multiple_of(o, PIECE), PIECE), cols] for o in offsets], axis=0)


def _scatter(ref, offsets, cols, val):
    for i, o in enumerate(offsets):
        ref[pl.ds(pl.multiple_of(o, PIECE), PIECE), cols] = val[i * PIECE:(i + 1) * PIECE]


def _attn_a_kernel(perm_ref, q_ref, k_ref, v_ref, o_ref,
                   qt_ref, kt_ref, vt_ref, acc_ref, lse_ref, *, heads):
    seq = q_ref.shape[0]
    n_groups = seq // GROUP
    perm = perm_ref[...]
    head_cols = [slice(h * HEAD_DIM_A, (h + 1) * HEAD_DIM_A) for h in range(heads)]
    all_lanes = slice(0, BLOCK)

    def transpose_groups(g, carry):
        rows = pl.ds(pl.multiple_of(g * GROUP, GROUP), GROUP)
        for src, dst in ((q_ref, qt_ref), (k_ref, kt_ref), (v_ref, vt_ref)):
            dst[rows, :] = _dot(perm, src[rows, :]).astype(BF16)
        return carry

    lax.fori_loop(0, n_groups, transpose_groups, 0)

    def i4_of(idx):
        return 64 * (idx >> 6) + 4 * (idx & 15) + ((idx >> 4) & 3)

    qi = lax.broadcasted_iota(jnp.int32, (BLOCK, 2 * BLOCK), 0)
    kj = lax.broadcasted_iota(jnp.int32, (BLOCK, 2 * BLOCK), 1)
    dist4 = BLOCK + i4_of(qi) - i4_of(kj)
    band4 = (dist4 >= 0) & (dist4 <= BLOCK)

    def dilation4(r4, carry):
        piece = [PIECE * (r4 + 4 * k) for k in range(4)]
        for n in range(seq // 4 // BLOCK):
            q_off = [GROUP * (2 * n + g) + p for g in range(2) for p in piece]
            if n == 0:
                k_off, mask = q_off, band4[:, BLOCK:]
            else:
                k_off = [o - 2 * GROUP for o in q_off] + q_off
                mask = band4
            for h, cols in enumerate(head_cols):
                m, l, acc = _softmax_block(_gather(qt_ref, q_off, cols),
                                           _gather(kt_ref, k_off, cols),
                                           _gather(vt_ref, k_off, cols), mask)
                _scatter(acc_ref, q_off, cols, acc * (1.0 / l))
                _scatter(lse_ref.at[h], q_off, all_lanes,
                         jnp.broadcast_to(m + jnp.log(l), (BLOCK, BLOCK)))
        return carry

    lax.fori_loop(0, 4, dilation4, 0)

    blocks16 = seq // PIECE // BLOCK
    groups_per_block = BLOCK // PIECE
    first16 = _band_mask(BLOCK, BLOCK)
    band = _band_mask(BLOCK)

    def dilation16(r, carry):
        for n in range(blocks16):
            q_off = [GROUP * (groups_per_block * n + g) + PIECE * r
                     for g in range(groups_per_block)]
            if n == 0:
                k_off, mask = q_off, first16
            else:
                k_off = [o - GROUP * groups_per_block for o in q_off] + q_off
                mask = band
            for h, cols in enumerate(head_cols):
                m, l, acc = _softmax_block(_gather(qt_ref, q_off, cols),
                                           _gather(kt_ref, k_off, cols),
                                           _gather(vt_ref, k_off, cols), mask)
                o, lse = _merge(_gather(acc_ref, q_off, cols),
                                _gather(lse_ref.at[h], q_off, all_lanes), m, l, acc)
                _scatter(acc_ref, q_off, cols, o)
                _scatter(lse_ref.at[h], q_off, all_lanes, jnp.broadcast_to(lse, (BLOCK, BLOCK)))
        return carry

    lax.fori_loop(0, PIECE, dilation16, 0, unroll=UNROLL)

    def split3(t):
        hi = t.astype(BF16)
        rest = t - hi.astype(F32)
        mid = rest.astype(BF16)
        return hi, mid, (rest - mid.astype(F32)).astype(BF16)

    def dilation1(g, carry):
        rows = pl.ds(pl.multiple_of(g * GROUP, GROUP), GROUP)
        o_prev = _dot(perm, acc_ref[rows, :].astype(BF16))
        lse_prev = [sum(_dot(perm, part) for part in split3(lse_ref[h, rows, :]))
                    for h in range(heads)]
        for j in range(GROUP // BLOCK):
            blk = (GROUP // BLOCK) * g + j
            r0 = pl.multiple_of(blk * BLOCK, BLOCK)
            p0 = pl.multiple_of(jnp.maximum(blk - 1, 0) * BLOCK, BLOCK)
            mask = _drop_missing_prev(band, blk == 0)
            local = slice(j * BLOCK, (j + 1) * BLOCK)
            for h, cols in enumerate(head_cols):
                k2 = jnp.concatenate([k_ref[pl.ds(p0, BLOCK), cols],
                                      k_ref[pl.ds(r0, BLOCK), cols]], axis=0)
                v2 = jnp.concatenate([v_ref[pl.ds(p0, BLOCK), cols],
                                      v_ref[pl.ds(r0, BLOCK), cols]], axis=0)
                m, l, acc = _softmax_block(q_ref[pl.ds(r0, BLOCK), cols], k2, v2, mask)
                o, _ = _merge(o_prev[local, cols], lse_prev[h][local], m, l, acc)
                o_ref[pl.ds(r0, BLOCK), cols] = o.astype(BF16)
        return carry

    lax.fori_loop(0, n_groups, dilation1, 0, unroll=UNROLL)


def _group_transpose_matrix():
    idx = jnp.arange(GROUP)
    return (idx[:, None] == PIECE * (idx[None, :] % PIECE) + idx[None, :] // PIECE).astype(BF16)


def _attn_a(qa, ka, va, batch, seq, heads=2):
    width = heads * HEAD_DIM_A
    view = lambda t: t.reshape(batch, seq, WIDTH_A)
    blk = pl.BlockSpec((None, seq, width), lambda b, h: (b, 0, h))
    o = pl.pallas_call(
        functools.partial(_attn_a_kernel, heads=heads),
        grid=(batch, N_HEADS_A // heads),
        in_specs=[pl.BlockSpec((GROUP, GROUP), lambda b, h: (0, 0)), blk, blk, blk],
        out_specs=blk,
        out_shape=jax.ShapeDtypeStruct((batch, seq, WIDTH_A), BF16),
        scratch_shapes=[pltpu.VMEM((seq, width), BF16)] * 3
        + [pltpu.VMEM((seq, width), F32), pltpu.VMEM((heads, seq, BLOCK), F32)],
        compiler_params=_params("arbitrary", "arbitrary"),
        name="attn_a",
    )(_group_transpose_matrix(), view(qa), view(ka), view(va))
    return o.reshape(batch * seq, WIDTH_A)


def _attn_b_kernel(sink_ref, q_ref, kvp_ref, kv_ref, o_ref, *, rows):
    stack = lambda t: jnp.concatenate([t, t], axis=0)
    band = _band_mask(BLOCK - 1)
    first_mask = stack(_drop_missing_prev(band, pl.program_id(1) == 0))
    band = stack(band)
    lane = lax.broadcasted_iota(jnp.int32, (BLOCK, BLOCK), 1)
    low = lane < HEAD_DIM_B
    low2 = lax.broadcasted_iota(jnp.int32, (2 * BLOCK, BLOCK), 1) < HEAD_DIM_B
    row2 = lax.broadcasted_iota(jnp.int32, (2 * BLOCK, 1), 0)
    heads_per_group = N_Q_HEADS_B // N_KV_HEADS_B
    for j in range(rows // BLOCK):
        r0 = j * BLOCK
        if j == 0:
            kv2 = jnp.concatenate([kvp_ref[...], kv_ref[:BLOCK, :]], axis=0)
            mask = first_mask
        else:
            kv2 = kv_ref[r0 - BLOCK:r0 + BLOCK, :]
            mask = band
        kv2 = kv2.astype(F32)
        k2, v2 = kv2[:, :KV_WIDTH_B], kv2[:, KV_WIDTH_B:]
        k2r, v2r = pltpu.roll(k2, HEAD_DIM_B, 1), pltpu.roll(v2, HEAD_DIM_B, 1)
        kdup = [jnp.where(low2, k2, k2r).astype(BF16), jnp.where(low2, k2r, k2).astype(BF16)]
        vdup = [jnp.where(low2, v2, v2r).astype(BF16), jnp.where(low2, v2r, v2).astype(BF16)]
        for t in range(WIDTH_B // BLOCK):
            g = (2 * t) // heads_per_group
            cols = slice(t * BLOCK, (t + 1) * BLOCK)
            q = q_ref[r0:r0 + BLOCK, cols]
            zero = jnp.zeros_like(q)
            q2 = jnp.concatenate([jnp.where(low, q, zero), jnp.where(low, zero, q)], axis=0)
            m, l, acc = _softmax_block(q2, kdup[g], vdup[g], mask)
            sink = jnp.where(row2 < BLOCK, sink_ref[2 * t], sink_ref[2 * t + 1])
            m2 = jnp.maximum(m, sink)
            c = jnp.exp(m - m2)
            den = l * c + jnp.exp(sink - m2)
            y = acc * (c / den)
            o_ref[r0:r0 + BLOCK, cols] = jnp.where(low, y[:BLOCK], y[BLOCK:]).astype(BF16)


def _attn_b(sinks, qb, kvb, batch, seq, rows=512):
    nblk = seq // rows
    prev_per_blk = rows // BLOCK
    view = lambda t: t.reshape(batch, seq, t.shape[-1])
    cur = lambda b, n: (b, n, 0)
    prev = lambda b, n: (b, jnp.maximum(n * prev_per_blk - 1, 0), 0)
    o = pl.pallas_call(
        functools.partial(_attn_b_kernel, rows=rows),
        grid=(batch, nblk),
        in_specs=[pl.BlockSpec(memory_space=pltpu.SMEM),
                  pl.BlockSpec((None, rows, WIDTH_B), cur),
                  pl.BlockSpec((None, BLOCK, 2 * KV_WIDTH_B), prev),
                  pl.BlockSpec((None, rows, 2 * KV_WIDTH_B), cur)],
        out_specs=pl.BlockSpec((None, rows, WIDTH_B), cur),
        out_shape=jax.ShapeDtypeStruct((batch, seq, WIDTH_B), BF16),
        compiler_params=_params("arbitrary", "arbitrary"),
        name="attn_b",
    )(sinks, view(qb), view(kvb), view(kvb))
    return o.reshape(batch * seq, WIDTH_B)


def _rms_gain(y, gain):
    return y * lax.rsqrt(jnp.mean(y * y, axis=-1, keepdims=True) + RMS_EPS) * gain


def _outproj_kernel(ya_ref, yb_ref, x_ref, w_ref, gna_ref, gnb_ref, g_ref, b_ref, out_ref):
    ya = _rms_gain(ya_ref[...].astype(F32), gna_ref[...]).astype(BF16)
    yb = _rms_gain(yb_ref[...].astype(F32), gnb_ref[...]).astype(BF16)
    y = _dot(ya, w_ref[:WIDTH_A, :]) + _dot(yb, w_ref[WIDTH_A:, :])
    out_ref[...] = _layernorm(ALPHA * x_ref[...] + y, g_ref[...], b_ref[...])


def _layer_vec(width, layer):
    return pl.BlockSpec((None, 1, width), lambda *_: (layer, 0, 0))


def _outproj(ya, yb, x, w, gna, gnb, g, b, layer, tm=512):
    m = x.shape[0]
    row = lambda i: (i, 0)
    return pl.pallas_call(
        _outproj_kernel,
        grid=(m // tm,),
        in_specs=[pl.BlockSpec((tm, WIDTH_A), row), pl.BlockSpec((tm, WIDTH_B), row),
                  pl.BlockSpec((tm, D_MODEL), row),
                  _single_buffered((None, WIDTH_A + WIDTH_B, D_MODEL), lambda i: (layer, 0, 0)),
                  _layer_vec(WIDTH_A, layer), _layer_vec(WIDTH_B, layer),
                  _layer_vec(D_MODEL, layer), _layer_vec(D_MODEL, layer)],
        out_specs=pl.BlockSpec((tm, D_MODEL), row),
        out_shape=jax.ShapeDtypeStruct((m, D_MODEL), F32),
        compiler_params=_params("arbitrary"),
        name="outproj",
    )(ya, yb, x, w, gna, gnb, g, b)


def _memkv_kernel(mem_ref, w_ref, kv_ref):
    kv_ref[...] = _dot(mem_ref[...].astype(BF16), w_ref[...].astype(BF16)).astype(BF16)


def _memkv(mem, w_mkv, tn=1024):
    rows = mem.shape[0]
    return pl.pallas_call(
        _memkv_kernel,
        grid=(DEPTH, 2 * D_MODEL // tn),
        in_specs=[pl.BlockSpec((rows, D_MODEL), lambda i, j: (0, 0)),
                  pl.BlockSpec((None, D_MODEL, tn), lambda i, j: (i, 0, j))],
        out_specs=pl.BlockSpec((None, rows, tn), lambda i, j: (i, 0, j)),
        out_shape=jax.ShapeDtypeStruct((DEPTH, rows, 2 * D_MODEL), BF16),
        compiler_params=_params("arbitrary", "arbitrary"),
        name="memkv",
    )(mem, w_mkv)


def _memattn_kernel(x_ref, wq_ref, k_ref, v_ref, wo_ref, g_ref, b_ref, out_ref):
    x = x_ref[...]
    q = (_dot(x.astype(BF16), wq_ref[...]) * MEM_HEAD_DIM ** -0.5).astype(BF16)
    heads = []
    for h in range(N_MEM_HEADS):
        cols = slice(h * MEM_HEAD_DIM, (h + 1) * MEM_HEAD_DIM)
        s = _dot_nt(q[:, cols], k_ref[:, cols])
        m = jnp.max(s, axis=-1, keepdims=True)
        p = jnp.exp(s - m)
        l = jnp.sum(p, axis=-1, keepdims=True)
        o = _dot(p.astype(BF16), v_ref[:, cols])
        heads.append((o * (1.0 / l)).astype(BF16))
    y = _dot(jnp.concatenate(heads, axis=1), wo_ref[...])
    out_ref[...] = _layernorm(ALPHA * x + y, g_ref[...], b_ref[...])


def _memattn(x, wq, kv, wo, g, b, batch, seq, layer, tm=512):
    n_mem = kv.shape[1] // batch
    per_batch = seq // tm
    row = lambda i: (i, 0)
    weight = _single_buffered((None, D_MODEL, D_MODEL), lambda i: (layer, 0, 0))
    kv_spec = lambda half: pl.BlockSpec((None, n_mem, D_MODEL),
                                        lambda i: (layer, i // per_batch, half))
    return pl.pallas_call(
        _memattn_kernel,
        grid=(batch * seq // tm,),
        in_specs=[pl.BlockSpec((tm, D_MODEL), row), weight, kv_spec(0), kv_spec(1), weight,
                  _layer_vec(D_MODEL, layer), _layer_vec(D_MODEL, layer)],
        out_specs=pl.BlockSpec((tm, D_MODEL), row),
        out_shape=jax.ShapeDtypeStruct((batch * seq, D_MODEL), F32),
        compiler_params=_params("arbitrary"),
        name="memattn",
    )(x, wq, kv, kv, wo, g, b)


def _mlp_kernel(x_ref, wu_ref, wd_ref, g_ref, b_ref, out_ref, xb_ref):
    j = pl.program_id(1)

    @pl.when(j == 0)
    def _():
        xb_ref[...] = x_ref[...].astype(BF16)
        out_ref[...] = jnp.zeros_like(out_ref)

    h = jnp.maximum(_dot(xb_ref[...], wu_ref[...]), 0.0)
    out_ref[...] += _dot((h * h).astype(BF16), wd_ref[...])

    @pl.when(j == pl.num_programs(1) - 1)
    def _():
        out_ref[...] = _layernorm(ALPHA * x_ref[...] + out_ref[...], g_ref[...], b_ref[...])


def _mlp(x, wu, wd, g, b, layer, tm=1024, tf=512):
    m = x.shape[0]
    row = lambda i, j: (i, 0)
    return pl.pallas_call(
        _mlp_kernel,
        grid=(m // tm, D_FF // tf),
        in_specs=[_single_buffered((tm, D_MODEL), row),
                  pl.BlockSpec((None, D_MODEL, tf), lambda i, j: (layer, 0, j)),
                  pl.BlockSpec((None, tf, D_MODEL), lambda i, j: (layer, j, 0)),
                  _layer_vec(D_MODEL, layer), _layer_vec(D_MODEL, layer)],
        out_specs=pl.BlockSpec((tm, D_MODEL), row),
        out_shape=jax.ShapeDtypeStruct((m, D_MODEL), F32),
        scratch_shapes=[pltpu.VMEM((tm, D_MODEL), BF16)],
        compiler_params=_params("arbitrary", "arbitrary"),
        name="mlp",
    )(x, wu, wd, g, b)


def _rope_lane_tables(positions, head_dim, rot_dim):
    half = rot_dim // 2
    inv_freq = ROPE_THETA ** (-jnp.arange(0, rot_dim, 2, dtype=F32) / rot_dim)
    ang = positions.astype(F32).reshape(-1, 1) * inv_freq
    cos, sin = jnp.cos(ang), jnp.sin(ang)
    rows = ang.shape[0]
    pad = jnp.zeros((rows, head_dim - rot_dim), F32)
    zero = jnp.zeros((rows, half), F32)
    reps = BLOCK // head_dim
    c = jnp.tile(jnp.concatenate([cos, cos, pad + 1.0], axis=1), (1, reps))
    s_up = jnp.tile(jnp.concatenate([zero, sin, pad], axis=1), (1, reps))
    s_dn = jnp.tile(jnp.concatenate([-sin, zero, pad], axis=1), (1, reps))
    return c, s_up, s_dn


def kernel(x, mem, positions, w_in, gn_a, gn_b, sinks, w_out, ln_mix_g, ln_mix_b, w_mq, w_mkv,
           w_mo, ln_mem_g, ln_mem_b, w_up, w_down, ln_ff_g, ln_ff_b):
    batch, seq, _ = x.shape
    tabs_a = _rope_lane_tables(positions, HEAD_DIM_A, ROT_DIM_A)
    tabs_b = _rope_lane_tables(positions, HEAD_DIM_B, ROT_DIM_B)
    w_in, w_out, w_mq, w_mo, w_up, w_down = (
        w.astype(BF16) for w in (w_in, w_out, w_mq, w_mo, w_up, w_down))
    gn_a, gn_b, ln_mix_g, ln_mix_b, ln_mem_g, ln_mem_b, ln_ff_g, ln_ff_b = (
        v.reshape(DEPTH, 1, -1)
        for v in (gn_a, gn_b, ln_mix_g, ln_mix_b, ln_mem_g, ln_mem_b, ln_ff_g, ln_ff_b))
    kv = _memkv(mem.reshape(-1, D_MODEL), w_mkv)
    h = x.reshape(batch * seq, D_MODEL)
    for i in range(DEPTH):
        qa, ka, va, qb, kvb = _inproj(h, w_in, i, tabs_a, tabs_b)
        ya = _attn_a(qa, ka, va, batch, seq)
        yb = _attn_b(sinks[i], qb, kvb, batch, seq)
        h = _outproj(ya, yb, h, w_out, gn_a, gn_b, ln_mix_g, ln_mix_b, i)
        h = _memattn(h, w_mq, kv, w_mo, ln_mem_g, ln_mem_b, batch, seq, i)
        h = _mlp(h, w_up, w_down, ln_ff_g, ln_ff_b, i)
    return h.reshape(batch, seq, D_MODEL)
```

```python
import functools
import math

import jax
import jax.numpy as jnp
from jax import lax
from jax.experimental import pallas as pl
from jax.experimental.pallas import tpu as pltpu

D_MODEL = 2048
DEPTH = 4
N_MEM_HEADS = 4
MEM_HEAD_DIM = D_MODEL // N_MEM_HEADS
HEAD_DIM_A = 128
N_HEADS_A = 8
WIDTH_A = N_HEADS_A * HEAD_DIM_A
HEAD_DIM_B = 64
N_Q_HEADS_B = 16
N_KV_HEADS_B = 2
WIDTH_B = N_Q_HEADS_B * HEAD_DIM_B
KV_WIDTH_B = N_KV_HEADS_B * HEAD_DIM_B
ROPE_THETA = 500000.0
ROT_DIM_A = HEAD_DIM_A // 4
ROT_DIM_B = HEAD_DIM_B // 4
D_FF = 4 * D_MODEL
BLOCK = 128
ALPHA = (2 * DEPTH) ** 0.25
LN_EPS = 1e-5
RMS_EPS = 1e-6
NEG_INF = -1e30
LOG2E = math.log2(math.e)

PIECE = 16
GROUP = PIECE * PIECE
UNROLL = 4

BF16 = jnp.bfloat16
F32 = jnp.float32

VMEM_LIMIT_BYTES = 56 * 1024 * 1024


def _params(*semantics):
    return pltpu.CompilerParams(dimension_semantics=semantics,
                                vmem_limit_bytes=VMEM_LIMIT_BYTES)


def _single_buffered(block_shape, index_map):
    return pl.BlockSpec(block_shape, index_map, pipeline_mode=pl.Buffered(1))


def _layernorm(z, g, b):
    mu = jnp.mean(z, axis=-1, keepdims=True)
    zc = z - mu
    var = jnp.mean(zc * zc, axis=-1, keepdims=True)
    return zc * lax.rsqrt(var + LN_EPS) * g + b


def _dot_nt(a, b):
    return lax.dot_general(a, b, (((1,), (1,)), ((), ())), preferred_element_type=F32)


def _dot(a, b):
    return jnp.dot(a, b, preferred_element_type=F32)


def _rope(t, c, s_up, s_dn, half):
    return (t * c + pltpu.roll(t, half, 1) * s_up
            + pltpu.roll(t, BLOCK - half, 1) * s_dn)


def _inproj_kernel(x_ref, w_ref, ca_ref, ua_ref, da_ref, cb_ref, ub_ref, db_ref,
                   qa_ref, ka_ref, va_ref, qb_ref, kvb_ref):
    x = x_ref[...].astype(BF16)
    ca, ua, da = ca_ref[...], ua_ref[...], da_ref[...]
    cb, ub, db = cb_ref[...], ub_ref[...], db_ref[...]
    scale_a = HEAD_DIM_A ** -0.5 * LOG2E
    scale_b = HEAD_DIM_B ** -0.5 * LOG2E
    chunk = 4 * BLOCK

    def proj(col, width):
        return _dot(x, w_ref[:, col:col + width])

    for c0 in range(0, WIDTH_A, chunk):
        hq = proj(c0, chunk)
        hk = proj(WIDTH_A + c0, chunk)
        for j in range(0, chunk, BLOCK):
            sl = slice(c0 + j, c0 + j + BLOCK)
            qa_ref[:, sl] = (_rope(hq[:, j:j + BLOCK], ca, ua, da, ROT_DIM_A // 2)
                             * scale_a).astype(BF16)
            ka_ref[:, sl] = _rope(hk[:, j:j + BLOCK], ca, ua, da,
                                  ROT_DIM_A // 2).astype(BF16)
        va_ref[:, c0:c0 + chunk] = proj(2 * WIDTH_A + c0, chunk).astype(BF16)
    for c0 in range(0, WIDTH_B, chunk):
        hq = proj(3 * WIDTH_A + c0, chunk)
        for j in range(0, chunk, BLOCK):
            qb_ref[:, c0 + j:c0 + j + BLOCK] = (
                _rope(hq[:, j:j + BLOCK], cb, ub, db, ROT_DIM_B // 2) * scale_b).astype(BF16)
    hkv = proj(3 * WIDTH_A + WIDTH_B, 2 * KV_WIDTH_B)
    kvb_ref[:, :KV_WIDTH_B] = _rope(hkv[:, :KV_WIDTH_B], cb, ub, db,
                                    ROT_DIM_B // 2).astype(BF16)
    kvb_ref[:, KV_WIDTH_B:] = hkv[:, KV_WIDTH_B:].astype(BF16)


def _inproj(x, w, layer, tabs_a, tabs_b, tm=512):
    m = x.shape[0]
    in_width = w.shape[-1]
    row = lambda i: (i, 0)
    tab_spec = pl.BlockSpec((tm, BLOCK), row)
    out = lambda width: jax.ShapeDtypeStruct((m, width), BF16)
    return pl.pallas_call(
        _inproj_kernel,
        grid=(m // tm,),
        in_specs=[pl.BlockSpec((tm, D_MODEL), row),
                  _single_buffered((None, D_MODEL, in_width), lambda i: (layer, 0, 0))]
        + [tab_spec] * 6,
        out_specs=[pl.BlockSpec((tm, WIDTH_A), row)] * 3
        + [pl.BlockSpec((tm, WIDTH_B), row), pl.BlockSpec((tm, 2 * KV_WIDTH_B), row)],
        out_shape=[out(WIDTH_A)] * 3 + [out(WIDTH_B), out(2 * KV_WIDTH_B)],
        compiler_params=_params("arbitrary"),
        name="inproj",
    )(x, w, *tabs_a, *tabs_b)


def _band_mask(max_dist):
    qi = lax.broadcasted_iota(jnp.int32, (BLOCK, 2 * BLOCK), 0) + BLOCK
    kj = lax.broadcasted_iota(jnp.int32, (BLOCK, 2 * BLOCK), 1)
    dist = qi - kj
    return (dist >= 0) & (dist <= max_dist)


def _softmax_block(q, k, v, bias):
    s = _dot_nt(q, k) + bias
    m = jnp.max(s, axis=-1, keepdims=True)
    p = jnp.exp2(s - m)
    l = jnp.sum(p, axis=-1, keepdims=True)
    return m, l, _dot(p.astype(BF16), v)


def _softmax_block_onto(q, k, v, bias, m_prev, l_prev, acc_prev):
    s = _dot_nt(q, k) + bias
    m = jnp.maximum(m_prev, jnp.max(s, axis=-1, keepdims=True))
    scale_prev = jnp.exp2(m_prev - m)
    p = jnp.exp2(s - jnp.concatenate([m] * (s.shape[1] // BLOCK), axis=1))
    l = scale_prev * l_prev + jnp.sum(p, axis=-1, keepdims=True)
    return m, l, scale_prev * acc_prev + _dot(p.astype(BF16), v)


def _gather(ref, offsets, cols):
    return jnp.concatenate(
        [ref[pl.ds(pl.multiple_of(o, PIECE), PIECE), cols] for o in offsets], axis=0)


def _scatter(ref, offsets, cols, val):
    for i, o in enumerate(offsets):
        ref[pl.ds(pl.multiple_of(o, PIECE), PIECE), cols] = val[i * PIECE:(i + 1) * PIECE]


def _attn_a_kernel(perm_ref, q_ref, k_ref, v_ref, o_ref,
                   qt_ref, kt_ref, vt_ref, acc4_ref, m4_ref, l4_ref, acc16_ref, m16_ref, l16_ref,
                   bias_ref, *, heads):
    seq = q_ref.shape[0]
    n_groups = seq // GROUP
    perm = perm_ref[...]
    head_cols = [slice(h * HEAD_DIM_A, (h + 1) * HEAD_DIM_A) for h in range(heads)]
    all_lanes = slice(0, BLOCK)
    tile = lambda col: jnp.broadcast_to(col, (BLOCK, BLOCK))

    def i4_of(idx):
        return 64 * (idx >> 6) + 4 * (idx & 15) + ((idx >> 4) & 3)

    qi = lax.broadcasted_iota(jnp.int32, (BLOCK, 2 * BLOCK), 0)
    kj = lax.broadcasted_iota(jnp.int32, (BLOCK, 2 * BLOCK), 1)
    dist4 = BLOCK + i4_of(qi) - i4_of(kj)
    for i, band in enumerate((_band_mask(BLOCK), (dist4 >= 0) & (dist4 <= BLOCK))):
        bias_ref[i] = jnp.where(band, 0.0, NEG_INF)
        bias_ref[2 + i] = jnp.where(band & (kj >= BLOCK), 0.0, NEG_INF)

    def transpose_groups(t, carry):
        for i in range(2):
            rows = pl.ds(pl.multiple_of((2 * t + i) * GROUP, GROUP), GROUP)
            for src, dst in ((q_ref, qt_ref), (k_ref, kt_ref), (v_ref, vt_ref)):
                dst[rows, :] = _dot(perm, src[rows, :]).astype(BF16)
        return carry

    lax.fori_loop(0, n_groups // 2, transpose_groups, 0)

    def dilation4(r4, carry):
        piece = [PIECE * (r4 + 4 * k) for k in range(4)]
        for n in range(seq // 4 // BLOCK):
            q_off = [GROUP * (2 * n + g) + p for g in range(2) for p in piece]
            if n == 0:
                k_off, bias = q_off + q_off, bias_ref[3]
            else:
                k_off, bias = [o - 2 * GROUP for o in q_off] + q_off, bias_ref[1]
            for h, cols in enumerate(head_cols):
                m, l, acc = _softmax_block(_gather(qt_ref, q_off, cols),
                                           _gather(kt_ref, k_off, cols),
                                           _gather(vt_ref, k_off, cols), bias)
                _scatter(acc4_ref, q_off, cols, acc)
                _scatter(m4_ref.at[h], q_off, all_lanes, tile(m))
                _scatter(l4_ref.at[h], q_off, all_lanes, tile(l))
        return carry

    lax.fori_loop(0, 4, dilation4, 0)

    groups_per_block = BLOCK // PIECE

    def dilation16(t, carry):
        for r, n in [(UNROLL * t + i, n) for i in range(UNROLL)
                     for n in range(seq // PIECE // BLOCK)]:
            q_off = [GROUP * (groups_per_block * n + g) + PIECE * r
                     for g in range(groups_per_block)]
            if n == 0:
                k_off, bias = q_off + q_off, bias_ref[2]
            else:
                k_off = [o - GROUP * groups_per_block for o in q_off] + q_off
                bias = bias_ref[0]
            for h, cols in enumerate(head_cols):
                m, l, acc = _softmax_block(_gather(qt_ref, q_off, cols),
                                           _gather(kt_ref, k_off, cols),
                                           _gather(vt_ref, k_off, cols), bias)
                _scatter(acc16_ref, q_off, cols, acc)
                _scatter(m16_ref.at[h], q_off, all_lanes, tile(m))
                _scatter(l16_ref.at[h], q_off, all_lanes, tile(l))
        return carry

    lax.fori_loop(0, PIECE // UNROLL, dilation16, 0)

    def dilation1(g, first):
        g = jnp.int32(g)
        rows = pl.ds(pl.multiple_of(g * GROUP, GROUP), GROUP)
        accs, stats = [], []
        for h, cols in enumerate(head_cols):
            m4, m16 = m4_ref[h, rows, :], m16_ref[h, rows, :]
            m = jnp.maximum(m4, m16).astype(BF16)
            w4, w16 = jnp.exp2(m4 - m.astype(F32)), jnp.exp2(m16 - m.astype(F32))
            l = w4 * l4_ref[h, rows, :] + w16 * l16_ref[h, rows, :]
            l_hi = l.astype(BF16)
            accs.append((w4 * acc4_ref[rows, cols] + w16 * acc16_ref[rows, cols]).astype(BF16))
            stats += [m, l_hi, (l - l_hi.astype(F32)).astype(BF16)]
        carry = _dot(perm, jnp.concatenate(accs + stats, axis=1))
        for j in range(GROUP // BLOCK):
            local = slice(j * BLOCK, (j + 1) * BLOCK)
            r0 = pl.multiple_of(g * GROUP + j * BLOCK, BLOCK)
            for h, cols in enumerate(head_cols):
                stats = heads * HEAD_DIM_A + 3 * BLOCK * h
                m_prev = carry[local, stats:stats + BLOCK]
                l_prev = (carry[local, stats + BLOCK:stats + 2 * BLOCK]
                          + carry[local, stats + 2 * BLOCK:stats + 3 * BLOCK])
                if first and j == 0:
                    p0, bias = (r0, r0), bias_ref[2]
                else:
                    p0 = pl.multiple_of(g * GROUP + (j - 1) * BLOCK, BLOCK)
                    p0, bias = (p0, r0), bias_ref[0]
                k2 = jnp.concatenate([k_ref[pl.ds(p, BLOCK), cols] for p in p0], axis=0)
                v2 = jnp.concatenate([v_ref[pl.ds(p, BLOCK), cols] for p in p0], axis=0)
                _, l, acc = _softmax_block_onto(q_ref[pl.ds(r0, BLOCK), cols], k2, v2, bias,
                                                m_prev, l_prev, carry[local, cols])
                o_ref[pl.ds(r0, BLOCK), cols] = (acc * (1.0 / l)).astype(BF16)

    def later_groups(t, carry):
        for i in range(3):
            dilation1(1 + 3 * t + i, False)
        return carry

    dilation1(0, True)
    lax.fori_loop(0, (n_groups - 1) // 3, later_groups, 0)


def _group_transpose_matrix():
    idx = jnp.arange(GROUP)
    return (idx[:, None] == PIECE * (idx[None, :] % PIECE) + idx[None, :] // PIECE).astype(BF16)


def _attn_a(qa, ka, va, batch, seq, heads=2):
    width = heads * HEAD_DIM_A
    view = lambda t: t.reshape(batch, seq, WIDTH_A)
    blk = pl.BlockSpec((None, seq, width), lambda b, h: (b, 0, h))
    stat = pltpu.VMEM((heads, seq, BLOCK), F32)
    o = pl.pallas_call(
        functools.partial(_attn_a_kernel, heads=heads),
        grid=(batch, N_HEADS_A // heads),
        in_specs=[pl.BlockSpec((GROUP, GROUP), lambda b, h: (0, 0)), blk, blk, blk],
        out_specs=blk,
        out_shape=jax.ShapeDtypeStruct((batch, seq, WIDTH_A), BF16),
        scratch_shapes=[pltpu.VMEM((seq, width), BF16)] * 3
        + [pltpu.VMEM((seq, width), F32), stat, stat] * 2
        + [pltpu.VMEM((4, BLOCK, 2 * BLOCK), F32)],
        compiler_params=_params("arbitrary", "arbitrary"),
        name="attn_a",
    )(_group_transpose_matrix(), view(qa), view(ka), view(va))
    return o.reshape(batch * seq, WIDTH_A)


def _attn_b_kernel(sink_ref, q_ref, kvp_ref, kv_ref, o_ref, *, rows):
    stack = lambda t: jnp.concatenate([t, t], axis=0)
    qi = lax.broadcasted_iota(jnp.int32, (BLOCK, BLOCK), 0)
    kj = lax.broadcasted_iota(jnp.int32, (BLOCK, BLOCK), 1)
    from_prev = stack(kj > qi)
    no_prev = jnp.where(pl.program_id(1) == 0, NEG_INF, 0.0)
    low = kj < HEAD_DIM_B
    low2 = stack(low)
    row2 = lax.broadcasted_iota(jnp.int32, (2 * BLOCK, 1), 0)
    heads_per_group = N_Q_HEADS_B // N_KV_HEADS_B
    for j in range(rows // BLOCK):
        r0 = j * BLOCK
        if j == 0:
            kv2 = jnp.concatenate([kvp_ref[...], kv_ref[:BLOCK, :]], axis=0)
        else:
            kv2 = kv_ref[r0 - BLOCK:r0 + BLOCK, :]
        kv2 = kv2.astype(F32)
        k2, v2 = kv2[:, :KV_WIDTH_B], kv2[:, KV_WIDTH_B:]
        k2r, v2r = pltpu.roll(k2, HEAD_DIM_B, 1), pltpu.roll(v2, HEAD_DIM_B, 1)
        kdup = [jnp.where(low2, k2, k2r).astype(BF16), jnp.where(low2, k2r, k2).astype(BF16)]
        vdup = [jnp.where(low2, v2, v2r).astype(BF16), jnp.where(low2, v2r, v2).astype(BF16)]
        for t in range(WIDTH_B // BLOCK):
            g = (2 * t) // heads_per_group
            cols = slice(t * BLOCK, (t + 1) * BLOCK)
            q = q_ref[r0:r0 + BLOCK, cols]
            zero = jnp.zeros_like(q)
            q2 = jnp.concatenate([jnp.where(low, q, zero), jnp.where(low, zero, q)], axis=0)
            s2 = _dot_nt(q2, kdup[g])
            s_prev = s2[:, :BLOCK] + no_prev if j == 0 else s2[:, :BLOCK]
            s = jnp.where(from_prev, s_prev, s2[:, BLOCK:])
            m = jnp.max(s, axis=-1, keepdims=True)
            p = jnp.exp2(s - m)
            l = jnp.sum(p, axis=-1, keepdims=True)
            p2 = jnp.concatenate([jnp.where(from_prev, p, 0.0), jnp.where(from_prev, 0.0, p)],
                                 axis=1)
            acc = _dot(p2.astype(BF16), vdup[g])
            sink = jnp.where(row2 < BLOCK, sink_ref[2 * t], sink_ref[2 * t + 1]) * LOG2E
            m2 = jnp.maximum(m, sink)
            c = jnp.exp2(m - m2)
            den = l * c + jnp.exp2(sink - m2)
            y = acc * (c / den)
            o_ref[r0:r0 + BLOCK, cols] = jnp.where(low, y[:BLOCK], y[BLOCK:]).astype(BF16)


def _attn_b(sinks, qb, kvb, batch, seq, rows=512):
    nblk = seq // rows
    prev_per_blk = rows // BLOCK
    view = lambda t: t.reshape(batch, seq, t.shape[-1])
    cur = lambda b, n: (b, n, 0)
    prev = lambda b, n: (b, jnp.maximum(n * prev_per_blk - 1, 0), 0)
    o = pl.pallas_call(
        functools.partial(_attn_b_kernel, rows=rows),
        grid=(batch, nblk),
        in_specs=[pl.BlockSpec(memory_space=pltpu.SMEM),
                  pl.BlockSpec((None, rows, WIDTH_B), cur),
                  pl.BlockSpec((None, BLOCK, 2 * KV_WIDTH_B), prev),
                  pl.BlockSpec((None, rows, 2 * KV_WIDTH_B), cur)],
        out_specs=pl.BlockSpec((None, rows, WIDTH_B), cur),
        out_shape=jax.ShapeDtypeStruct((batch, seq, WIDTH_B), BF16),
        compiler_params=_params("arbitrary", "arbitrary"),
        name="attn_b",
    )(sinks, view(qb), view(kvb), view(kvb))
    return o.reshape(batch * seq, WIDTH_B)


def _rms_gain(y, gain):
    return y * lax.rsqrt(jnp.mean(y * y, axis=-1, keepdims=True) + RMS_EPS) * gain


def _outproj_kernel(ya_ref, yb_ref, x_ref, w_ref, gna_ref, gnb_ref, g_ref, b_ref, out_ref):
    ya = _rms_gain(ya_ref[...].astype(F32), gna_ref[...]).astype(BF16)
    yb = _rms_gain(yb_ref[...].astype(F32), gnb_ref[...]).astype(BF16)
    y = _dot(ya, w_ref[:WIDTH_A, :]) + _dot(yb, w_ref[WIDTH_A:, :])
    out_ref[...] = _layernorm(ALPHA * x_ref[...] + y, g_ref[...], b_ref[...])


def _layer_vec(width, layer):
    return pl.BlockSpec((None, 1, width), lambda *_: (layer, 0, 0))


def _outproj(ya, yb, x, w, gna, gnb, g, b, layer, tm=512):
    m = x.shape[0]
    row = lambda i: (i, 0)
    return pl.pallas_call(
        _outproj_kernel,
        grid=(m // tm,),
        in_specs=[pl.BlockSpec((tm, WIDTH_A), row), pl.BlockSpec((tm, WIDTH_B), row),
                  pl.BlockSpec((tm, D_MODEL), row),
                  _single_buffered((None, WIDTH_A + WIDTH_B, D_MODEL), lambda i: (layer, 0, 0)),
                  _layer_vec(WIDTH_A, layer), _layer_vec(WIDTH_B, layer),
                  _layer_vec(D_MODEL, layer), _layer_vec(D_MODEL, layer)],
        out_specs=pl.BlockSpec((tm, D_MODEL), row),
        out_shape=jax.ShapeDtypeStruct((m, D_MODEL), F32),
        compiler_params=_params("arbitrary"),
        name="outproj",
    )(ya, yb, x, w, gna, gnb, g, b)


def _memkv_kernel(mem_ref, w_ref, kv_ref):
    kv_ref[...] = _dot(mem_ref[...].astype(BF16), w_ref[...].astype(BF16)).astype(BF16)


def _memkv(mem, w_mkv, tn=1024):
    rows = mem.shape[0]
    return pl.pallas_call(
        _memkv_kernel,
        grid=(DEPTH, 2 * D_MODEL // tn),
        in_specs=[pl.BlockSpec((rows, D_MODEL), lambda i, j: (0, 0)),
                  pl.BlockSpec((None, D_MODEL, tn), lambda i, j: (i, 0, j))],
        out_specs=pl.BlockSpec((None, rows, tn), lambda i, j: (i, 0, j)),
        out_shape=jax.ShapeDtypeStruct((DEPTH, rows, 2 * D_MODEL), BF16),
        compiler_params=_params("arbitrary", "arbitrary"),
        name="memkv",
    )(mem, w_mkv)


def _memattn_kernel(x_ref, wq_ref, k_ref, v_ref, wo_ref, g_ref, b_ref, out_ref):
    x = x_ref[...]
    q = (_dot(x.astype(BF16), wq_ref[...]) * MEM_HEAD_DIM ** -0.5).astype(BF16)
    heads = []
    for h in range(N_MEM_HEADS):
        cols = slice(h * MEM_HEAD_DIM, (h + 1) * MEM_HEAD_DIM)
        s = _dot_nt(q[:, cols], k_ref[:, cols])
        m = jnp.max(s, axis=-1, keepdims=True)
        p = jnp.exp(s - m)
        l = jnp.sum(p, axis=-1, keepdims=True)
        o = _dot(p.astype(BF16), v_ref[:, cols])
        heads.append((o * (1.0 / l)).astype(BF16))
    y = _dot(jnp.concatenate(heads, axis=1), wo_ref[...])
    out_ref[...] = _layernorm(ALPHA * x + y, g_ref[...], b_ref[...])


def _memattn(x, wq, kv, wo, g, b, batch, seq, layer, tm=512):
    n_mem = kv.shape[1] // batch
    per_batch = seq // tm
    row = lambda i: (i, 0)
    weight = _single_buffered((None, D_MODEL, D_MODEL), lambda i: (layer, 0, 0))
    kv_spec = lambda half: pl.BlockSpec((None, n_mem, D_MODEL),
                                        lambda i: (layer, i // per_batch, half))
    return pl.pallas_call(
        _memattn_kernel,
        grid=(batch * seq // tm,),
        in_specs=[pl.BlockSpec((tm, D_MODEL), row), weight, kv_spec(0), kv_spec(1), weight,
                  _layer_vec(D_MODEL, layer), _layer_vec(D_MODEL, layer)],
        out_specs=pl.BlockSpec((tm, D_MODEL), row),
        out_shape=jax.ShapeDtypeStruct((batch * seq, D_MODEL), F32),
        compiler_params=_params("arbitrary"),
        name="memattn",
    )(x, wq, kv, kv, wo, g, b)


def _mlp_kernel(x_ref, wu_ref, wd_ref, g_ref, b_ref, out_ref, xb_ref):
    j = pl.program_id(1)

    @pl.when(j == 0)
    def _():
        xb_ref[...] = x_ref[...].astype(BF16)
        out_ref[...] = jnp.zeros_like(out_ref)

    h = jnp.maximum(_dot(xb_ref[...], wu_ref[...].astype(BF16)), 0.0)
    out_ref[...] += _dot((h * h).astype(BF16), wd_ref[...].astype(BF16))

    @pl.when(j == pl.num_programs(1) - 1)
    def _():
        out_ref[...] = _layernorm(ALPHA * x_ref[...] + out_ref[...], g_ref[...], b_ref[...])


def _mlp(x, wu, wd, g, b, layer, tm=1024, tf=512):
    m = x.shape[0]
    row = lambda i, j: (i, 0)
    return pl.pallas_call(
        _mlp_kernel,
        grid=(m // tm, D_FF // tf),
        in_specs=[_single_buffered((tm, D_MODEL), row),
                  pl.BlockSpec((None, D_MODEL, tf), lambda i, j: (layer, 0, j)),
                  pl.BlockSpec((None, tf, D_MODEL), lambda i, j: (layer, j, 0)),
                  _layer_vec(D_MODEL, layer), _layer_vec(D_MODEL, layer)],
        out_specs=pl.BlockSpec((tm, D_MODEL), row),
        out_shape=jax.ShapeDtypeStruct((m, D_MODEL), F32),
        scratch_shapes=[pltpu.VMEM((tm, D_MODEL), BF16)],
        compiler_params=_params("arbitrary", "arbitrary"),
        name="mlp",
    )(x, wu, wd, g, b)


def _rope_lane_tables(positions, head_dim, rot_dim):
    half = rot_dim // 2
    inv_freq = ROPE_THETA ** (-jnp.arange(0, rot_dim, 2, dtype=F32) / rot_dim)
    ang = positions.astype(F32).reshape(-1, 1) * inv_freq
    cos, sin = jnp.cos(ang), jnp.sin(ang)
    rows = ang.shape[0]
    pad = jnp.zeros((rows, head_dim - rot_dim), F32)
    zero = jnp.zeros((rows, half), F32)
    reps = BLOCK // head_dim
    c = jnp.tile(jnp.concatenate([cos, cos, pad + 1.0], axis=1), (1, reps))
    s_up = jnp.tile(jnp.concatenate([zero, sin, pad], axis=1), (1, reps))
    s_dn = jnp.tile(jnp.concatenate([-sin, zero, pad], axis=1), (1, reps))
    return c, s_up, s_dn


def kernel(x, mem, positions, w_in, gn_a, gn_b, sinks, w_out, ln_mix_g, ln_mix_b, w_mq, w_mkv,
           w_mo, ln_mem_g, ln_mem_b, w_up, w_down, ln_ff_g, ln_ff_b):
    batch, seq, _ = x.shape
    tabs_a = _rope_lane_tables(positions, HEAD_DIM_A, ROT_DIM_A)
    tabs_b = _rope_lane_tables(positions, HEAD_DIM_B, ROT_DIM_B)
    w_in, w_out, w_mq, w_mo = (w.astype(BF16) for w in (w_in, w_out, w_mq, w_mo))
    gn_a, gn_b, ln_mix_g, ln_mix_b, ln_mem_g, ln_mem_b, ln_ff_g, ln_ff_b = (
        v.reshape(DEPTH, 1, -1)
        for v in (gn_a, gn_b, ln_mix_g, ln_mix_b, ln_mem_g, ln_mem_b, ln_ff_g, ln_ff_b))
    kv = _memkv(mem.reshape(-1, D_MODEL), w_mkv)
    h = x.reshape(batch * seq, D_MODEL)
    for i in range(DEPTH):
        qa, ka, va, qb, kvb = _inproj(h, w_in, i, tabs_a, tabs_b)
        ya = _attn_a(qa, ka, va, batch, seq)
        yb = _attn_b(sinks[i], qb, kvb, batch, seq)
        h = _outproj(ya, yb, h, w_out, gn_a, gn_b, ln_mix_g, ln_mix_b, i)
        h = _memattn(h, w_mq, kv, w_mo, ln_mem_g, ln_mem_b, batch, seq, i)
        h = _mlp(h, w_up, w_down, ln_ff_g, ln_ff_b, i)
    return h.reshape(batch, seq, D_MODEL)
```

```python
import functools
import math

import jax
import jax.numpy as jnp
from jax import lax
from jax.experimental import pallas as pl
from jax.experimental.pallas import tpu as pltpu

D_MODEL = 2048
DEPTH = 4
N_MEM_HEADS = 4
MEM_HEAD_DIM = D_MODEL // N_MEM_HEADS
HEAD_DIM_A = 128
N_HEADS_A = 8
WIDTH_A = N_HEADS_A * HEAD_DIM_A
HEAD_DIM_B = 64
N_Q_HEADS_B = 16
N_KV_HEADS_B = 2
WIDTH_B = N_Q_HEADS_B * HEAD_DIM_B
KV_WIDTH_B = N_KV_HEADS_B * HEAD_DIM_B
ROPE_THETA = 500000.0
ROT_DIM_A = HEAD_DIM_A // 4
ROT_DIM_B = HEAD_DIM_B // 4
D_FF = 4 * D_MODEL
BLOCK = 128
ALPHA = (2 * DEPTH) ** 0.25
LN_EPS = 1e-5
RMS_EPS = 1e-6
NEG_INF = -1e30
LOG2E = math.log2(math.e)

PIECE = 16
GROUP = PIECE * PIECE
UNROLL = 4

BF16 = jnp.bfloat16
F32 = jnp.float32

VMEM_LIMIT_BYTES = 58 * 1024 * 1024


def _params(*semantics):
    return pltpu.CompilerParams(dimension_semantics=semantics,
                                vmem_limit_bytes=VMEM_LIMIT_BYTES)


def _single_buffered(block_shape, index_map):
    return pl.BlockSpec(block_shape, index_map, pipeline_mode=pl.Buffered(1))


def _resident_weight(w):
    return _single_buffered((None,) + w.shape[1:], lambda *_: (0, 0, 0))


def _layernorm(z, g, b):
    mu = jnp.mean(z, axis=-1, keepdims=True)
    zc = z - mu
    var = jnp.mean(zc * zc, axis=-1, keepdims=True)
    return zc * lax.rsqrt(var + LN_EPS) * g + b


def _dot_nt(a, b):
    return lax.dot_general(a, b, (((1,), (1,)), ((), ())), preferred_element_type=F32)


def _dot(a, b):
    return jnp.dot(a, b, preferred_element_type=F32)


def _rope(t, c, s_up, s_dn, half):
    return (t * c + pltpu.roll(t, half, 1) * s_up
            + pltpu.roll(t, BLOCK - half, 1) * s_dn)


def _inproj_kernel(x_ref, w_ref, ca_ref, ua_ref, da_ref, cb_ref, ub_ref, db_ref,
                   qa_ref, ka_ref, va_ref, qb_ref, kvb_ref):
    x = x_ref[...].astype(BF16)
    ca, ua, da = ca_ref[...], ua_ref[...], da_ref[...]
    cb, ub, db = cb_ref[...], ub_ref[...], db_ref[...]
    scale_a = HEAD_DIM_A ** -0.5 * LOG2E
    scale_b = HEAD_DIM_B ** -0.5 * LOG2E
    chunk = 4 * BLOCK

    def proj(col, width):
        return _dot(x, w_ref[:, col:col + width])

    for c0 in range(0, WIDTH_A, chunk):
        hq = proj(c0, chunk)
        hk = proj(WIDTH_A + c0, chunk)
        for j in range(0, chunk, BLOCK):
            sl = slice(c0 + j, c0 + j + BLOCK)
            qa_ref[:, sl] = (_rope(hq[:, j:j + BLOCK], ca, ua, da, ROT_DIM_A // 2)
                             * scale_a).astype(BF16)
            ka_ref[:, sl] = _rope(hk[:, j:j + BLOCK], ca, ua, da,
                                  ROT_DIM_A // 2).astype(BF16)
        va_ref[:, c0:c0 + chunk] = proj(2 * WIDTH_A + c0, chunk).astype(BF16)
    for c0 in range(0, WIDTH_B, chunk):
        hq = proj(3 * WIDTH_A + c0, chunk)
        for j in range(0, chunk, BLOCK):
            qb_ref[:, c0 + j:c0 + j + BLOCK] = (
                _rope(hq[:, j:j + BLOCK], cb, ub, db, ROT_DIM_B // 2) * scale_b).astype(BF16)
    hkv = proj(3 * WIDTH_A + WIDTH_B, 2 * KV_WIDTH_B)
    kvb_ref[:, :KV_WIDTH_B] = _rope(hkv[:, :KV_WIDTH_B], cb, ub, db,
                                    ROT_DIM_B // 2).astype(BF16)
    kvb_ref[:, KV_WIDTH_B:] = hkv[:, KV_WIDTH_B:].astype(BF16)


def _inproj(x, w, tabs_a, tabs_b, tm=512):
    m = x.shape[0]
    row = lambda i: (i, 0)
    tab_spec = pl.BlockSpec((tm, BLOCK), row)
    out = lambda width: jax.ShapeDtypeStruct((m, width), BF16)
    return pl.pallas_call(
        _inproj_kernel,
        grid=(m // tm,),
        in_specs=[pl.BlockSpec((tm, D_MODEL), row), _resident_weight(w)] + [tab_spec] * 6,
        out_specs=[pl.BlockSpec((tm, WIDTH_A), row)] * 3
        + [pl.BlockSpec((tm, WIDTH_B), row), pl.BlockSpec((tm, 2 * KV_WIDTH_B), row)],
        out_shape=[out(WIDTH_A)] * 3 + [out(WIDTH_B), out(2 * KV_WIDTH_B)],
        compiler_params=_params("arbitrary"),
        name="inproj",
    )(x, w, *tabs_a, *tabs_b)


def _band_mask(max_dist):
    qi = lax.broadcasted_iota(jnp.int32, (BLOCK, 2 * BLOCK), 0) + BLOCK
    kj = lax.broadcasted_iota(jnp.int32, (BLOCK, 2 * BLOCK), 1)
    dist = qi - kj
    return (dist >= 0) & (dist <= max_dist)


def _softmax_block(q, k, v, bias):
    s = _dot_nt(q, k) + bias
    m = jnp.max(s, axis=-1, keepdims=True)
    p = jnp.exp2(s - m)
    l = jnp.sum(p, axis=-1, keepdims=True)
    return m, l, _dot(p.astype(BF16), v)


def _softmax_block_onto(q, k, v, bias, m_prev, l_prev, acc_prev, bf16_shift=False):
    s = _dot_nt(q, k) + bias
    m = jnp.maximum(m_prev, jnp.max(s, axis=-1, keepdims=True))
    if bf16_shift:
        m = m.astype(BF16).astype(F32)
    scale_prev = jnp.exp2(m_prev - m)
    p = jnp.exp2(s - jnp.concatenate([m] * (s.shape[1] // BLOCK), axis=1))
    l = scale_prev * l_prev + jnp.sum(p, axis=-1, keepdims=True)
    return m, l, scale_prev * acc_prev + _dot(p.astype(BF16), v)


def _gather(ref, offsets, cols):
    return jnp.concatenate(
        [ref[pl.ds(pl.multiple_of(o, PIECE), PIECE), cols] for o in offsets], axis=0)


def _scatter(ref, offsets, cols, val):
    for i, o in enumerate(offsets):
        ref[pl.ds(pl.multiple_of(o, PIECE), PIECE), cols] = val[i * PIECE:(i + 1) * PIECE]


def _carry_cols(heads, h):
    stat0 = heads * HEAD_DIM_A + 3 * BLOCK * h
    return h * HEAD_DIM_A, stat0, stat0 + BLOCK, stat0 + 2 * BLOCK


def _attn_a_kernel(perm_ref, q_ref, k_ref, v_ref, o_ref,
                   qt_ref, kt_ref, vt_ref, acc4_ref, m4_ref, l4_ref, carry_ref, bias_ref,
                   *, heads):
    seq = q_ref.shape[0]
    n_groups = seq // GROUP
    perm = perm_ref[...]
    head_cols = [slice(h * HEAD_DIM_A, (h + 1) * HEAD_DIM_A) for h in range(heads)]
    all_lanes = slice(0, BLOCK)
    tile = lambda col: jnp.broadcast_to(col, (BLOCK, BLOCK))

    def i4_of(idx):
        return 64 * (idx >> 6) + 4 * (idx & 15) + ((idx >> 4) & 3)

    qi = lax.broadcasted_iota(jnp.int32, (BLOCK, 2 * BLOCK), 0)
    kj = lax.broadcasted_iota(jnp.int32, (BLOCK, 2 * BLOCK), 1)
    dist4 = BLOCK + i4_of(qi) - i4_of(kj)
    for i, band in enumerate((_band_mask(BLOCK), (dist4 >= 0) & (dist4 <= BLOCK))):
        bias_ref[i] = jnp.where(band, 0.0, NEG_INF)
        bias_ref[2 + i] = jnp.where(band & (kj >= BLOCK), 0.0, NEG_INF)

    def transpose_groups(t, carry):
        for i in range(2):
            rows = pl.ds(pl.multiple_of((2 * t + i) * GROUP, GROUP), GROUP)
            for src, dst in ((q_ref, qt_ref), (k_ref, kt_ref), (v_ref, vt_ref)):
                dst[rows, :] = _dot(perm, src[rows, :]).astype(BF16)
        return carry

    lax.fori_loop(0, n_groups // 2, transpose_groups, 0)

    def dilation4(r4, carry):
        piece = [PIECE * (r4 + 4 * k) for k in range(4)]
        for n in range(seq // 4 // BLOCK):
            q_off = [GROUP * (2 * n + g) + p for g in range(2) for p in piece]
            if n == 0:
                k_off, bias = q_off + q_off, bias_ref[3]
            else:
                k_off, bias = [o - 2 * GROUP for o in q_off] + q_off, bias_ref[1]
            for h, cols in enumerate(head_cols):
                m, l, acc = _softmax_block(_gather(qt_ref, q_off, cols),
                                           _gather(kt_ref, k_off, cols),
                                           _gather(vt_ref, k_off, cols), bias)
                _scatter(acc4_ref, q_off, cols, acc)
                _scatter(m4_ref.at[h], q_off, all_lanes, tile(m))
                _scatter(l4_ref.at[h], q_off, all_lanes, tile(l))
        return carry

    lax.fori_loop(0, 4, dilation4, 0)

    groups_per_block = BLOCK // PIECE

    def dilation16(t, carry):
        for r, n in [(UNROLL * t + i, n) for i in range(UNROLL)
                     for n in range(seq // PIECE // BLOCK)]:
            q_off = [GROUP * (groups_per_block * n + g) + PIECE * r
                     for g in range(groups_per_block)]
            if n == 0:
                k_off, bias = q_off + q_off, bias_ref[2]
            else:
                k_off = [o - GROUP * groups_per_block for o in q_off] + q_off
                bias = bias_ref[0]
            for h, cols in enumerate(head_cols):
                m, l, acc = _softmax_block_onto(
                    _gather(qt_ref, q_off, cols), _gather(kt_ref, k_off, cols),
                    _gather(vt_ref, k_off, cols), bias,
                    _gather(m4_ref.at[h], q_off, all_lanes),
                    _gather(l4_ref.at[h], q_off, all_lanes),
                    _gather(acc4_ref, q_off, cols), bf16_shift=True)
                l_hi = l.astype(BF16)
                for c0, val in zip(_carry_cols(heads, h),
                                   (acc, m, l_hi, l - l_hi.astype(F32))):
                    _scatter(carry_ref, q_off, slice(c0, c0 + BLOCK), val.astype(BF16))
        return carry

    lax.fori_loop(0, PIECE // UNROLL, dilation16, 0)

    def dilation1(g, first):
        g = jnp.int32(g)
        rows = pl.ds(pl.multiple_of(g * GROUP, GROUP), GROUP)
        carry = _dot(perm, carry_ref[rows, :])
        for j in range(GROUP // BLOCK):
            local = slice(j * BLOCK, (j + 1) * BLOCK)
            r0 = pl.multiple_of(g * GROUP + j * BLOCK, BLOCK)
            for h, cols in enumerate(head_cols):
                acc_prev, m_prev, l_hi, l_lo = (carry[local, c0:c0 + BLOCK]
                                                for c0 in _carry_cols(heads, h))
                if first and j == 0:
                    p0, bias = (r0, r0), bias_ref[2]
                else:
                    p0 = pl.multiple_of(g * GROUP + (j - 1) * BLOCK, BLOCK)
                    p0, bias = (p0, r0), bias_ref[0]
                k2 = jnp.concatenate([k_ref[pl.ds(p, BLOCK), cols] for p in p0], axis=0)
                v2 = jnp.concatenate([v_ref[pl.ds(p, BLOCK), cols] for p in p0], axis=0)
                _, l, acc = _softmax_block_onto(q_ref[pl.ds(r0, BLOCK), cols], k2, v2, bias,
                                                m_prev, l_hi + l_lo, acc_prev)
                o_ref[pl.ds(r0, BLOCK), cols] = (acc * (1.0 / l)).astype(BF16)

    def later_groups(t, carry):
        for i in range(3):
            dilation1(1 + 3 * t + i, False)
        return carry

    dilation1(0, True)
    lax.fori_loop(0, (n_groups - 1) // 3, later_groups, 0)


def _group_transpose_matrix():
    idx = jnp.arange(GROUP)
    return (idx[:, None] == PIECE * (idx[None, :] % PIECE) + idx[None, :] // PIECE).astype(BF16)


def _attn_a(qa, ka, va, batch, seq, heads=2):
    width = heads * HEAD_DIM_A
    view = lambda t: t.reshape(batch, seq, WIDTH_A)
    blk = pl.BlockSpec((None, seq, width), lambda b, h: (b, 0, h))
    stat = pltpu.VMEM((heads, seq, BLOCK), F32)
    o = pl.pallas_call(
        functools.partial(_attn_a_kernel, heads=heads),
        grid=(batch, N_HEADS_A // heads),
        in_specs=[pl.BlockSpec((GROUP, GROUP), lambda b, h: (0, 0)), blk, blk, blk],
        out_specs=blk,
        out_shape=jax.ShapeDtypeStruct((batch, seq, WIDTH_A), BF16),
        scratch_shapes=[pltpu.VMEM((seq, width), BF16)] * 3
        + [pltpu.VMEM((seq, width), F32), stat, stat,
           pltpu.VMEM((seq, width + 3 * BLOCK * heads), BF16),
           pltpu.VMEM((4, BLOCK, 2 * BLOCK), F32)],
        compiler_params=_params("arbitrary", "arbitrary"),
        name="attn_a",
    )(_group_transpose_matrix(), view(qa), view(ka), view(va))
    return o.reshape(batch * seq, WIDTH_A)


def _attn_b_kernel(sink_ref, q_ref, kvp_ref, kv_ref, o_ref, *, rows):
    stack = lambda t: jnp.concatenate([t, t], axis=0)
    qi = lax.broadcasted_iota(jnp.int32, (BLOCK, BLOCK), 0)
    kj = lax.broadcasted_iota(jnp.int32, (BLOCK, BLOCK), 1)
    from_prev = stack(kj > qi)
    no_prev = jnp.where(pl.program_id(1) == 0, NEG_INF, 0.0)
    low = kj < HEAD_DIM_B
    low2 = stack(low)
    row2 = lax.broadcasted_iota(jnp.int32, (2 * BLOCK, 1), 0)
    heads_per_group = N_Q_HEADS_B // N_KV_HEADS_B
    for j in range(rows // BLOCK):
        r0 = j * BLOCK
        if j == 0:
            kv2 = jnp.concatenate([kvp_ref[...], kv_ref[:BLOCK, :]], axis=0)
        else:
            kv2 = kv_ref[r0 - BLOCK:r0 + BLOCK, :]
        kv2 = kv2.astype(F32)
        k2, v2 = kv2[:, :KV_WIDTH_B], kv2[:, KV_WIDTH_B:]
        k2r, v2r = pltpu.roll(k2, HEAD_DIM_B, 1), pltpu.roll(v2, HEAD_DIM_B, 1)
        kdup = [jnp.where(low2, k2, k2r).astype(BF16), jnp.where(low2, k2r, k2).astype(BF16)]
        vdup = [jnp.where(low2, v2, v2r).astype(BF16), jnp.where(low2, v2r, v2).astype(BF16)]
        for t in range(WIDTH_B // BLOCK):
            g = (2 * t) // heads_per_group
            cols = slice(t * BLOCK, (t + 1) * BLOCK)
            q = q_ref[r0:r0 + BLOCK, cols]
            zero = jnp.zeros_like(q)
            q2 = jnp.concatenate([jnp.where(low, q, zero), jnp.where(low, zero, q)], axis=0)
            s2 = _dot_nt(q2, kdup[g])
            s_prev = s2[:, :BLOCK] + no_prev if j == 0 else s2[:, :BLOCK]
            s = jnp.where(from_prev, s_prev, s2[:, BLOCK:])
            m = jnp.max(s, axis=-1, keepdims=True)
            p = jnp.exp2(s - m)
            l = jnp.sum(p, axis=-1, keepdims=True)
            p2 = jnp.concatenate([jnp.where(from_prev, p, 0.0), jnp.where(from_prev, 0.0, p)],
                                 axis=1)
            acc = _dot(p2.astype(BF16), vdup[g])
            sink = jnp.where(row2 < BLOCK, sink_ref[2 * t], sink_ref[2 * t + 1]) * LOG2E
            m2 = jnp.maximum(m, sink)
            c = jnp.exp2(m - m2)
            den = l * c + jnp.exp2(sink - m2)
            y = acc * (c / den)
            o_ref[r0:r0 + BLOCK, cols] = jnp.where(low, y[:BLOCK], y[BLOCK:]).astype(BF16)


def _attn_b(sinks, qb, kvb, batch, seq, rows=512):
    nblk = seq // rows
    prev_per_blk = rows // BLOCK
    view = lambda t: t.reshape(batch, seq, t.shape[-1])
    cur = lambda b, n: (b, n, 0)
    prev = lambda b, n: (b, jnp.maximum(n * prev_per_blk - 1, 0), 0)
    o = pl.pallas_call(
        functools.partial(_attn_b_kernel, rows=rows),
        grid=(batch, nblk),
        in_specs=[pl.BlockSpec(memory_space=pltpu.SMEM),
                  pl.BlockSpec((None, rows, WIDTH_B), cur),
                  pl.BlockSpec((None, BLOCK, 2 * KV_WIDTH_B), prev),
                  pl.BlockSpec((None, rows, 2 * KV_WIDTH_B), cur)],
        out_specs=pl.BlockSpec((None, rows, WIDTH_B), cur),
        out_shape=jax.ShapeDtypeStruct((batch, seq, WIDTH_B), BF16),
        compiler_params=_params("arbitrary", "arbitrary"),
        name="attn_b",
    )(sinks, view(qb), view(kvb), view(kvb))
    return o.reshape(batch * seq, WIDTH_B)


def _rms_gain(y, gain):
    return y * lax.rsqrt(jnp.mean(y * y, axis=-1, keepdims=True) + RMS_EPS) * gain


def _outproj_kernel(ya_ref, yb_ref, x_ref, w_ref, gna_ref, gnb_ref, g_ref, b_ref, out_ref):
    ya = _rms_gain(ya_ref[...].astype(F32), gna_ref[...]).astype(BF16)
    yb = _rms_gain(yb_ref[...].astype(F32), gnb_ref[...]).astype(BF16)
    y = _dot(ya, w_ref[:WIDTH_A, :]) + _dot(yb, w_ref[WIDTH_A:, :])
    out_ref[...] = _layernorm(ALPHA * x_ref[...] + y, g_ref[...], b_ref[...])


def _layer_vec(width, layer):
    return pl.BlockSpec((None, 1, width), lambda *_: (layer, 0, 0))


def _outproj(ya, yb, x, w, gna, gnb, g, b, layer, tm=512):
    m = x.shape[0]
    row = lambda i: (i, 0)
    return pl.pallas_call(
        _outproj_kernel,
        grid=(m // tm,),
        in_specs=[pl.BlockSpec((tm, WIDTH_A), row), pl.BlockSpec((tm, WIDTH_B), row),
                  pl.BlockSpec((tm, D_MODEL), row),
                  _resident_weight(w),
                  _layer_vec(WIDTH_A, layer), _layer_vec(WIDTH_B, layer),
                  _layer_vec(D_MODEL, layer), _layer_vec(D_MODEL, layer)],
        out_specs=pl.BlockSpec((tm, D_MODEL), row),
        out_shape=jax.ShapeDtypeStruct((m, D_MODEL), F32),
        compiler_params=_params("arbitrary"),
        name="outproj",
    )(ya, yb, x, w, gna, gnb, g, b)


def _memkv_kernel(mem_ref, w_ref, kv_ref):
    kv_ref[...] = _dot(mem_ref[...].astype(BF16), w_ref[...].astype(BF16)).astype(BF16)


def _memkv(mem, w_mkv, tn=1024):
    rows = mem.shape[0]
    return pl.pallas_call(
        _memkv_kernel,
        grid=(DEPTH, 2 * D_MODEL // tn),
        in_specs=[pl.BlockSpec((rows, D_MODEL), lambda i, j: (0, 0)),
                  pl.BlockSpec((None, D_MODEL, tn), lambda i, j: (i, 0, j))],
        out_specs=pl.BlockSpec((None, rows, tn), lambda i, j: (i, 0, j)),
        out_shape=jax.ShapeDtypeStruct((DEPTH, rows, 2 * D_MODEL), BF16),
        compiler_params=_params("arbitrary", "arbitrary"),
        name="memkv",
    )(mem, w_mkv)


def _memattn_kernel(x_ref, wq_ref, k_ref, v_ref, wo_ref, g_ref, b_ref, out_ref):
    x = x_ref[...]
    q = (_dot(x.astype(BF16), wq_ref[...]) * MEM_HEAD_DIM ** -0.5).astype(BF16)
    heads = []
    for h in range(N_MEM_HEADS):
        cols = slice(h * MEM_HEAD_DIM, (h + 1) * MEM_HEAD_DIM)
        s = _dot_nt(q[:, cols], k_ref[:, cols])
        m = jnp.max(s, axis=-1, keepdims=True)
        p = jnp.exp(s - m)
        l = jnp.sum(p, axis=-1, keepdims=True)
        o = _dot(p.astype(BF16), v_ref[:, cols])
        heads.append((o * (1.0 / l)).astype(BF16))
    y = _dot(jnp.concatenate(heads, axis=1), wo_ref[...])
    out_ref[...] = _layernorm(ALPHA * x + y, g_ref[...], b_ref[...])


def _memattn(x, wq, kv, wo, g, b, batch, seq, layer, tm=512):
    n_mem = kv.shape[1] // batch
    per_batch = seq // tm
    row = lambda i: (i, 0)
    weight = _resident_weight(wq)
    kv_spec = lambda half: pl.BlockSpec((None, n_mem, D_MODEL),
                                        lambda i: (layer, i // per_batch, half))
    return pl.pallas_call(
        _memattn_kernel,
        grid=(batch * seq // tm,),
        in_specs=[pl.BlockSpec((tm, D_MODEL), row), weight, kv_spec(0), kv_spec(1), weight,
                  _layer_vec(D_MODEL, layer), _layer_vec(D_MODEL, layer)],
        out_specs=pl.BlockSpec((tm, D_MODEL), row),
        out_shape=jax.ShapeDtypeStruct((batch * seq, D_MODEL), F32),
        compiler_params=_params("arbitrary"),
        name="memattn",
    )(x, wq, kv, kv, wo, g, b)


def _mlp_kernel(*refs, n_round):
    x_ref, wu_ref, wd_ref, g_ref, b_ref = refs[:5]
    round_in, out_ref = refs[5:5 + n_round], refs[5 + n_round]
    round_out, xb_ref = refs[6 + n_round:6 + 2 * n_round], refs[-1]
    j = pl.program_id(1)
    for src, dst in zip(round_in, round_out):
        dst[...] = src[...].astype(BF16)

    @pl.when(j == 0)
    def _():
        xb_ref[...] = x_ref[...].astype(BF16)
        out_ref[...] = jnp.zeros_like(out_ref)

    h = jnp.maximum(_dot(xb_ref[...], wu_ref[...].astype(BF16)), 0.0)
    out_ref[...] += _dot((h * h).astype(BF16), wd_ref[...].astype(BF16))

    @pl.when(j == pl.num_programs(1) - 1)
    def _():
        out_ref[...] = _layernorm(ALPHA * x_ref[...] + out_ref[...], g_ref[...], b_ref[...])


def _mlp(x, wu, wd, g, b, layer, round_next=(), tm=1024, tf=512):
    m = x.shape[0]
    chunks = D_FF // tf
    steps = (m // tm) * chunks
    row = lambda i, j: (i, 0)
    slice_rows = [w.shape[1] // steps for w in round_next]
    assert all(r * steps == w.shape[1] and r % PIECE == 0
               for r, w in zip(slice_rows, round_next))
    outs = pl.pallas_call(
        functools.partial(_mlp_kernel, n_round=len(round_next)),
        grid=(m // tm, chunks),
        in_specs=[_single_buffered((tm, D_MODEL), row),
                  pl.BlockSpec((None, D_MODEL, tf), lambda i, j: (layer, 0, j)),
                  pl.BlockSpec((None, tf, D_MODEL), lambda i, j: (layer, j, 0)),
                  _layer_vec(D_MODEL, layer), _layer_vec(D_MODEL, layer)]
        + [pl.BlockSpec((None, r, w.shape[2]), lambda i, j: (layer + 1, i * chunks + j, 0))
           for r, w in zip(slice_rows, round_next)],
        out_specs=[pl.BlockSpec((tm, D_MODEL), row)]
        + [pl.BlockSpec((None, r, w.shape[2]), lambda i, j: (0, i * chunks + j, 0))
           for r, w in zip(slice_rows, round_next)],
        out_shape=[jax.ShapeDtypeStruct((m, D_MODEL), F32)]
        + [jax.ShapeDtypeStruct((1,) + w.shape[1:], BF16) for w in round_next],
        scratch_shapes=[pltpu.VMEM((tm, D_MODEL), BF16)],
        compiler_params=_params("arbitrary", "arbitrary"),
        name="mlp",
    )(x, wu, wd, g, b, *round_next)
    return outs[0], outs[1:]


def _rope_lane_tables(positions, head_dim, rot_dim):
    half = rot_dim // 2
    inv_freq = ROPE_THETA ** (-jnp.arange(0, rot_dim, 2, dtype=F32) / rot_dim)
    ang = positions.astype(F32).reshape(-1, 1) * inv_freq
    cos, sin = jnp.cos(ang), jnp.sin(ang)
    rows = ang.shape[0]
    pad = jnp.zeros((rows, head_dim - rot_dim), F32)
    zero = jnp.zeros((rows, half), F32)
    reps = BLOCK // head_dim
    c = jnp.tile(jnp.concatenate([cos, cos, pad + 1.0], axis=1), (1, reps))
    s_up = jnp.tile(jnp.concatenate([zero, sin, pad], axis=1), (1, reps))
    s_dn = jnp.tile(jnp.concatenate([-sin, zero, pad], axis=1), (1, reps))
    return c, s_up, s_dn


def kernel(x, mem, positions, w_in, gn_a, gn_b, sinks, w_out, ln_mix_g, ln_mix_b, w_mq, w_mkv,
           w_mo, ln_mem_g, ln_mem_b, w_up, w_down, ln_ff_g, ln_ff_b):
    batch, seq, _ = x.shape
    tabs_a = _rope_lane_tables(positions, HEAD_DIM_A, ROT_DIM_A)
    tabs_b = _rope_lane_tables(positions, HEAD_DIM_B, ROT_DIM_B)
    resident = (w_in, w_out, w_mq, w_mo)
    rounded = [w[:1].astype(BF16) for w in resident]
    gn_a, gn_b, ln_mix_g, ln_mix_b, ln_mem_g, ln_mem_b, ln_ff_g, ln_ff_b = (
        v.reshape(DEPTH, 1, -1)
        for v in (gn_a, gn_b, ln_mix_g, ln_mix_b, ln_mem_g, ln_mem_b, ln_ff_g, ln_ff_b))
    kv = _memkv(mem.reshape(-1, D_MODEL), w_mkv)
    h = x.reshape(batch * seq, D_MODEL)
    for i in range(DEPTH):
        w_in_i, w_out_i, w_mq_i, w_mo_i = rounded
        qa, ka, va, qb, kvb = _inproj(h, w_in_i, tabs_a, tabs_b)
        ya = _attn_a(qa, ka, va, batch, seq)
        yb = _attn_b(sinks[i], qb, kvb, batch, seq)
        h = _outproj(ya, yb, h, w_out_i, gn_a, gn_b, ln_mix_g, ln_mix_b, i)
        h = _memattn(h, w_mq_i, kv, w_mo_i, ln_mem_g, ln_mem_b, batch, seq, i)
        h, rounded = _mlp(h, w_up, w_down, ln_ff_g, ln_ff_b, i,
                          round_next=resident if i + 1 < DEPTH else ())
    return h.reshape(batch, seq, D_MODEL)
```

```python
import functools
import math

import jax
import jax.numpy as jnp
from jax import lax
from jax.experimental import pallas as pl
from jax.experimental.pallas import tpu as pltpu

D_MODEL = 2048
DEPTH = 4
N_MEM_HEADS = 4
MEM_HEAD_DIM = D_MODEL // N_MEM_HEADS
HEAD_DIM_A = 128
N_HEADS_A = 8
WIDTH_A = N_HEADS_A * HEAD_DIM_A
HEAD_DIM_B = 64
N_Q_HEADS_B = 16
N_KV_HEADS_B = 2
WIDTH_B = N_Q_HEADS_B * HEAD_DIM_B
KV_WIDTH_B = N_KV_HEADS_B * HEAD_DIM_B
ROPE_THETA = 500000.0
ROT_DIM_A = HEAD_DIM_A // 4
ROT_DIM_B = HEAD_DIM_B // 4
D_FF = 4 * D_MODEL
BLOCK = 128
ALPHA = (2 * DEPTH) ** 0.25
LN_EPS = 1e-5
RMS_EPS = 1e-6
NEG_INF = -1e30
LOG2E = math.log2(math.e)

PIECE = 16
GROUP = PIECE * PIECE
UNROLL = 4

BF16 = jnp.bfloat16
F32 = jnp.float32

VMEM_LIMIT_BYTES = 58 * 1024 * 1024


def _params(*semantics):
    return pltpu.CompilerParams(dimension_semantics=semantics,
                                vmem_limit_bytes=VMEM_LIMIT_BYTES)


def _single_buffered(block_shape, index_map):
    return pl.BlockSpec(block_shape, index_map, pipeline_mode=pl.Buffered(1))


def _resident_weight(w):
    return _single_buffered((None,) + w.shape[1:], lambda *_: (0, 0, 0))


def _round_job(stacks, layer, steps):
    rows = [w.shape[1] // steps for w in stacks]
    assert all(r * steps == w.shape[1] and r % PIECE == 0 for r, w in zip(rows, stacks))
    blocks = [(None, r, w.shape[2]) for r, w in zip(rows, stacks)]
    return ([pl.BlockSpec(blk, lambda i: (layer, i, 0)) for blk in blocks],
            [pl.BlockSpec(blk, lambda i: (0, i, 0)) for blk in blocks],
            [jax.ShapeDtypeStruct((1,) + w.shape[1:], BF16) for w in stacks])


def _round_slices(src_refs, dst_refs):
    for src, dst in zip(src_refs, dst_refs):
        dst[...] = src[...].astype(BF16)


def _round_kernel(*refs):
    _round_slices(refs[:len(refs) // 2], refs[len(refs) // 2:])


def _round_layer(stacks, layer, steps=8):
    in_specs, out_specs, out_shape = _round_job(stacks, layer, steps)
    return pl.pallas_call(
        _round_kernel, grid=(steps,), in_specs=in_specs, out_specs=out_specs,
        out_shape=out_shape, compiler_params=_params("arbitrary"), name="round_weights",
    )(*stacks)


def _layernorm(z, g, b):
    mu = jnp.mean(z, axis=-1, keepdims=True)
    zc = z - mu
    var = jnp.mean(zc * zc, axis=-1, keepdims=True)
    return zc * lax.rsqrt(var + LN_EPS) * g + b


def _dot_nt(a, b):
    return lax.dot_general(a, b, (((1,), (1,)), ((), ())), preferred_element_type=F32)


def _dot(a, b):
    return jnp.dot(a, b, preferred_element_type=F32)


def _rope(t, c, s_up, s_dn, half):
    return (t * c + pltpu.roll(t, half, 1) * s_up
            + pltpu.roll(t, BLOCK - half, 1) * s_dn)


def _inproj_kernel(*refs, n_round):
    x_ref, w_ref, ca_ref, ua_ref, da_ref, cb_ref, ub_ref, db_ref = refs[:8]
    qa_ref, ka_ref, va_ref, qb_ref, kvb_ref = refs[8 + n_round:13 + n_round]
    _round_slices(refs[8:8 + n_round], refs[13 + n_round:])
    x = x_ref[...].astype(BF16)
    ca, ua, da = ca_ref[...], ua_ref[...], da_ref[...]
    cb, ub, db = cb_ref[...], ub_ref[...], db_ref[...]
    scale_a = HEAD_DIM_A ** -0.5 * LOG2E
    scale_b = HEAD_DIM_B ** -0.5 * LOG2E
    chunk = 4 * BLOCK

    def proj(col, width):
        return _dot(x, w_ref[:, col:col + width])

    for c0 in range(0, WIDTH_A, chunk):
        hq = proj(c0, chunk)
        hk = proj(WIDTH_A + c0, chunk)
        for j in range(0, chunk, BLOCK):
            sl = slice(c0 + j, c0 + j + BLOCK)
            qa_ref[:, sl] = (_rope(hq[:, j:j + BLOCK], ca, ua, da, ROT_DIM_A // 2)
                             * scale_a).astype(BF16)
            ka_ref[:, sl] = _rope(hk[:, j:j + BLOCK], ca, ua, da,
                                  ROT_DIM_A // 2).astype(BF16)
        va_ref[:, c0:c0 + chunk] = proj(2 * WIDTH_A + c0, chunk).astype(BF16)
    for c0 in range(0, WIDTH_B, chunk):
        hq = proj(3 * WIDTH_A + c0, chunk)
        for j in range(0, chunk, BLOCK):
            qb_ref[:, c0 + j:c0 + j + BLOCK] = (
                _rope(hq[:, j:j + BLOCK], cb, ub, db, ROT_DIM_B // 2) * scale_b).astype(BF16)
    hkv = proj(3 * WIDTH_A + WIDTH_B, 2 * KV_WIDTH_B)
    kvb_ref[:, :KV_WIDTH_B] = _rope(hkv[:, :KV_WIDTH_B], cb, ub, db,
                                    ROT_DIM_B // 2).astype(BF16)
    kvb_ref[:, KV_WIDTH_B:] = hkv[:, KV_WIDTH_B:].astype(BF16)


def _inproj(x, w, tabs_a, tabs_b, round_stacks=(), round_layer=0, tm=512):
    m = x.shape[0]
    row = lambda i: (i, 0)
    tab_spec = pl.BlockSpec((tm, BLOCK), row)
    out = lambda width: jax.ShapeDtypeStruct((m, width), BF16)
    round_in, round_out, round_shape = _round_job(round_stacks, round_layer, m // tm)
    outs = pl.pallas_call(
        functools.partial(_inproj_kernel, n_round=len(round_stacks)),
        grid=(m // tm,),
        in_specs=[pl.BlockSpec((tm, D_MODEL), row), _resident_weight(w)] + [tab_spec] * 6
        + round_in,
        out_specs=[pl.BlockSpec((tm, WIDTH_A), row)] * 3
        + [pl.BlockSpec((tm, WIDTH_B), row), pl.BlockSpec((tm, 2 * KV_WIDTH_B), row)]
        + round_out,
        out_shape=[out(WIDTH_A)] * 3 + [out(WIDTH_B), out(2 * KV_WIDTH_B)] + round_shape,
        compiler_params=_params("arbitrary"),
        name="inproj",
    )(x, w, *tabs_a, *tabs_b, *round_stacks)
    return outs[:5], outs[5:]


def _band_mask(max_dist):
    qi = lax.broadcasted_iota(jnp.int32, (BLOCK, 2 * BLOCK), 0) + BLOCK
    kj = lax.broadcasted_iota(jnp.int32, (BLOCK, 2 * BLOCK), 1)
    dist = qi - kj
    return (dist >= 0) & (dist <= max_dist)


def _softmax_block(q, k, v, bias):
    s = _dot_nt(q, k) + bias
    m = jnp.max(s, axis=-1, keepdims=True)
    p = jnp.exp2(s - m)
    l = jnp.sum(p, axis=-1, keepdims=True)
    return m, l, _dot(p.astype(BF16), v)


def _softmax_block_onto(q, k, v, bias, m_prev, l_prev, acc_prev):
    s = _dot_nt(q, k) + bias
    m = jnp.maximum(m_prev, jnp.max(s, axis=-1, keepdims=True))
    scale_prev = jnp.exp2(m_prev - m)
    p = jnp.exp2(s - jnp.concatenate([m] * (s.shape[1] // BLOCK), axis=1))
    l = scale_prev * l_prev + jnp.sum(p, axis=-1, keepdims=True)
    return m, l, scale_prev * acc_prev + _dot(p.astype(BF16), v)


def _gather(ref, offsets, cols):
    return jnp.concatenate(
        [ref[pl.ds(pl.multiple_of(o, PIECE), PIECE), cols] for o in offsets], axis=0)


def _scatter(ref, offsets, cols, val):
    for i, o in enumerate(offsets):
        ref[pl.ds(pl.multiple_of(o, PIECE), PIECE), cols] = val[i * PIECE:(i + 1) * PIECE]


def _attn_a_kernel(perm_ref, q_ref, k_ref, v_ref, o_ref,
                   qt_ref, kt_ref, vt_ref, acc4_ref, m4_ref, l4_ref, acc16_ref, m16_ref, l16_ref,
                   bias_ref, *, heads):
    seq = q_ref.shape[0]
    n_groups = seq // GROUP
    perm = perm_ref[...]
    head_cols = [slice(h * HEAD_DIM_A, (h + 1) * HEAD_DIM_A) for h in range(heads)]
    all_lanes = slice(0, BLOCK)
    tile = lambda col: jnp.broadcast_to(col, (BLOCK, BLOCK))

    def i4_of(idx):
        return 64 * (idx >> 6) + 4 * (idx & 15) + ((idx >> 4) & 3)

    qi = lax.broadcasted_iota(jnp.int32, (BLOCK, 2 * BLOCK), 0)
    kj = lax.broadcasted_iota(jnp.int32, (BLOCK, 2 * BLOCK), 1)
    dist4 = BLOCK + i4_of(qi) - i4_of(kj)
    for i, band in enumerate((_band_mask(BLOCK), (dist4 >= 0) & (dist4 <= BLOCK))):
        bias_ref[i] = jnp.where(band, 0.0, NEG_INF)
        bias_ref[2 + i] = jnp.where(band & (kj >= BLOCK), 0.0, NEG_INF)

    def transpose_groups(t, carry):
        for i in range(2):
            rows = pl.ds(pl.multiple_of((2 * t + i) * GROUP, GROUP), GROUP)
            for src, dst in ((q_ref, qt_ref), (k_ref, kt_ref), (v_ref, vt_ref)):
                dst[rows, :] = _dot(perm, src[rows, :]).astype(BF16)
        return carry

    lax.fori_loop(0, n_groups // 2, transpose_groups, 0)

    def dilation4(r4, carry):
        piece = [PIECE * (r4 + 4 * k) for k in range(4)]
        for n in range(seq // 4 // BLOCK):
            q_off = [GROUP * (2 * n + g) + p for g in range(2) for p in piece]
            if n == 0:
                k_off, bias = q_off + q_off, bias_ref[3]
            else:
                k_off, bias = [o - 2 * GROUP for o in q_off] + q_off, bias_ref[1]
            for h, cols in enumerate(head_cols):
                m, l, acc = _softmax_block(_gather(qt_ref, q_off, cols),
                                           _gather(kt_ref, k_off, cols),
                                           _gather(vt_ref, k_off, cols), bias)
                _scatter(acc4_ref, q_off, cols, acc)
                _scatter(m4_ref.at[h], q_off, all_lanes, tile(m))
                _scatter(l4_ref.at[h], q_off, all_lanes, tile(l))
        return carry

    lax.fori_loop(0, 4, dilation4, 0)

    groups_per_block = BLOCK // PIECE

    def dilation16(t, carry):
        for r, n in [(UNROLL * t + i, n) for i in range(UNROLL)
                     for n in range(seq // PIECE // BLOCK)]:
            q_off = [GROUP * (groups_per_block * n + g) + PIECE * r
                     for g in range(groups_per_block)]
            if n == 0:
                k_off, bias = q_off + q_off, bias_ref[2]
            else:
                k_off = [o - GROUP * groups_per_block for o in q_off] + q_off
                bias = bias_ref[0]
            for h, cols in enumerate(head_cols):
                m, l, acc = _softmax_block(_gather(qt_ref, q_off, cols),
                                           _gather(kt_ref, k_off, cols),
                                           _gather(vt_ref, k_off, cols), bias)
                _scatter(acc16_ref, q_off, cols, acc)
                _scatter(m16_ref.at[h], q_off, all_lanes, tile(m))
                _scatter(l16_ref.at[h], q_off, all_lanes, tile(l))
        return carry

    lax.fori_loop(0, PIECE // UNROLL, dilation16, 0)

    def dilation1(g, first):
        g = jnp.int32(g)
        rows = pl.ds(pl.multiple_of(g * GROUP, GROUP), GROUP)
        accs, stats = [], []
        for h, cols in enumerate(head_cols):
            m4, m16 = m4_ref[h, rows, :], m16_ref[h, rows, :]
            m = jnp.maximum(m4, m16).astype(BF16)
            w4, w16 = jnp.exp2(m4 - m.astype(F32)), jnp.exp2(m16 - m.astype(F32))
            l = w4 * l4_ref[h, rows, :] + w16 * l16_ref[h, rows, :]
            l_hi = l.astype(BF16)
            accs.append((w4 * acc4_ref[rows, cols] + w16 * acc16_ref[rows, cols]).astype(BF16))
            stats += [m, l_hi, (l - l_hi.astype(F32)).astype(BF16)]
        carry = _dot(perm, jnp.concatenate(accs + stats, axis=1))
        for j in range(GROUP // BLOCK):
            local = slice(j * BLOCK, (j + 1) * BLOCK)
            r0 = pl.multiple_of(g * GROUP + j * BLOCK, BLOCK)
            for h, cols in enumerate(head_cols):
                stats = heads * HEAD_DIM_A + 3 * BLOCK * h
                m_prev = carry[local, stats:stats + BLOCK]
                l_prev = (carry[local, stats + BLOCK:stats + 2 * BLOCK]
                          + carry[local, stats + 2 * BLOCK:stats + 3 * BLOCK])
                if first and j == 0:
                    p0, bias = (r0, r0), bias_ref[2]
                else:
                    p0 = pl.multiple_of(g * GROUP + (j - 1) * BLOCK, BLOCK)
                    p0, bias = (p0, r0), bias_ref[0]
                k2 = jnp.concatenate([k_ref[pl.ds(p, BLOCK), cols] for p in p0], axis=0)
                v2 = jnp.concatenate([v_ref[pl.ds(p, BLOCK), cols] for p in p0], axis=0)
                _, l, acc = _softmax_block_onto(q_ref[pl.ds(r0, BLOCK), cols], k2, v2, bias,
                                                m_prev, l_prev, carry[local, cols])
                o_ref[pl.ds(r0, BLOCK), cols] = (acc * (1.0 / l)).astype(BF16)

    def later_groups(t, carry):
        for i in range(3):
            dilation1(1 + 3 * t + i, False)
        return carry

    dilation1(0, True)
    lax.fori_loop(0, (n_groups - 1) // 3, later_groups, 0)


def _group_transpose_matrix():
    idx = jnp.arange(GROUP)
    return (idx[:, None] == PIECE * (idx[None, :] % PIECE) + idx[None, :] // PIECE).astype(BF16)


def _attn_a(qa, ka, va, batch, seq, heads=2):
    width = heads * HEAD_DIM_A
    view = lambda t: t.reshape(batch, seq, WIDTH_A)
    blk = pl.BlockSpec((None, seq, width), lambda b, h: (b, 0, h))
    stat = pltpu.VMEM((heads, seq, BLOCK), F32)
    o = pl.pallas_call(
        functools.partial(_attn_a_kernel, heads=heads),
        grid=(batch, N_HEADS_A // heads),
        in_specs=[pl.BlockSpec((GROUP, GROUP), lambda b, h: (0, 0)), blk, blk, blk],
        out_specs=blk,
        out_shape=jax.ShapeDtypeStruct((batch, seq, WIDTH_A), BF16),
        scratch_shapes=[pltpu.VMEM((seq, width), BF16)] * 3
        + [pltpu.VMEM((seq, width), F32), stat, stat] * 2
        + [pltpu.VMEM((4, BLOCK, 2 * BLOCK), F32)],
        compiler_params=_params("arbitrary", "arbitrary"),
        name="attn_a",
    )(_group_transpose_matrix(), view(qa), view(ka), view(va))
    return o.reshape(batch * seq, WIDTH_A)


def _attn_b_kernel(sink_ref, q_ref, kvp_ref, kv_ref, o_ref, *, rows):
    stack = lambda t: jnp.concatenate([t, t], axis=0)
    qi = lax.broadcasted_iota(jnp.int32, (BLOCK, BLOCK), 0)
    kj = lax.broadcasted_iota(jnp.int32, (BLOCK, BLOCK), 1)
    from_prev = stack(kj > qi)
    no_prev = jnp.where(pl.program_id(1) == 0, NEG_INF, 0.0)
    low = kj < HEAD_DIM_B
    low2 = stack(low)
    row2 = lax.broadcasted_iota(jnp.int32, (2 * BLOCK, 1), 0)
    heads_per_group = N_Q_HEADS_B // N_KV_HEADS_B
    for j in range(rows // BLOCK):
        r0 = j * BLOCK
        if j == 0:
            kv2 = jnp.concatenate([kvp_ref[...], kv_ref[:BLOCK, :]], axis=0)
        else:
            kv2 = kv_ref[r0 - BLOCK:r0 + BLOCK, :]
        kv2 = kv2.astype(F32)
        k2, v2 = kv2[:, :KV_WIDTH_B], kv2[:, KV_WIDTH_B:]
        k2r, v2r = pltpu.roll(k2, HEAD_DIM_B, 1), pltpu.roll(v2, HEAD_DIM_B, 1)
        kdup = [jnp.where(low2, k2, k2r).astype(BF16), jnp.where(low2, k2r, k2).astype(BF16)]
        vdup = [jnp.where(low2, v2, v2r).astype(BF16), jnp.where(low2, v2r, v2).astype(BF16)]
        for t in range(WIDTH_B // BLOCK):
            g = (2 * t) // heads_per_group
            cols = slice(t * BLOCK, (t + 1) * BLOCK)
            q = q_ref[r0:r0 + BLOCK, cols]
            zero = jnp.zeros_like(q)
            q2 = jnp.concatenate([jnp.where(low, q, zero), jnp.where(low, zero, q)], axis=0)
            s2 = _dot_nt(q2, kdup[g])
            s_prev = s2[:, :BLOCK] + no_prev if j == 0 else s2[:, :BLOCK]
            s = jnp.where(from_prev, s_prev, s2[:, BLOCK:])
            m = jnp.max(s, axis=-1, keepdims=True)
            p = jnp.exp2(s - m)
            l = jnp.sum(p, axis=-1, keepdims=True)
            p2 = jnp.concatenate([jnp.where(from_prev, p, 0.0), jnp.where(from_prev, 0.0, p)],
                                 axis=1)
            acc = _dot(p2.astype(BF16), vdup[g])
            sink = jnp.where(row2 < BLOCK, sink_ref[2 * t], sink_ref[2 * t + 1]) * LOG2E
            m2 = jnp.maximum(m, sink)
            c = jnp.exp2(m - m2)
            den = l * c + jnp.exp2(sink - m2)
            y = acc * (c / den)
            o_ref[r0:r0 + BLOCK, cols] = jnp.where(low, y[:BLOCK], y[BLOCK:]).astype(BF16)


def _attn_b(sinks, qb, kvb, batch, seq, rows=512):
    nblk = seq // rows
    prev_per_blk = rows // BLOCK
    view = lambda t: t.reshape(batch, seq, t.shape[-1])
    cur = lambda b, n: (b, n, 0)
    prev = lambda b, n: (b, jnp.maximum(n * prev_per_blk - 1, 0), 0)
    o = pl.pallas_call(
        functools.partial(_attn_b_kernel, rows=rows),
        grid=(batch, nblk),
        in_specs=[pl.BlockSpec(memory_space=pltpu.SMEM),
                  pl.BlockSpec((None, rows, WIDTH_B), cur),
                  pl.BlockSpec((None, BLOCK, 2 * KV_WIDTH_B), prev),
                  pl.BlockSpec((None, rows, 2 * KV_WIDTH_B), cur)],
        out_specs=pl.BlockSpec((None, rows, WIDTH_B), cur),
        out_shape=jax.ShapeDtypeStruct((batch, seq, WIDTH_B), BF16),
        compiler_params=_params("arbitrary", "arbitrary"),
        name="attn_b",
    )(sinks, view(qb), view(kvb), view(kvb))
    return o.reshape(batch * seq, WIDTH_B)


def _rms_gain(y, gain):
    return y * lax.rsqrt(jnp.mean(y * y, axis=-1, keepdims=True) + RMS_EPS) * gain


def _outproj_kernel(*refs, n_round):
    ya_ref, yb_ref, x_ref, w_ref, gna_ref, gnb_ref, g_ref, b_ref = refs[:8]
    out_ref = refs[8 + n_round]
    _round_slices(refs[8:8 + n_round], refs[9 + n_round:])
    ya = _rms_gain(ya_ref[...].astype(F32), gna_ref[...]).astype(BF16)
    yb = _rms_gain(yb_ref[...].astype(F32), gnb_ref[...]).astype(BF16)
    y = _dot(ya, w_ref[:WIDTH_A, :]) + _dot(yb, w_ref[WIDTH_A:, :])
    out_ref[...] = _layernorm(ALPHA * x_ref[...] + y, g_ref[...], b_ref[...])


def _layer_vec(width, layer):
    return pl.BlockSpec((None, 1, width), lambda *_: (layer, 0, 0))


def _outproj(ya, yb, x, w, gna, gnb, g, b, layer, round_stacks=(), tm=512):
    m = x.shape[0]
    row = lambda i: (i, 0)
    round_in, round_out, round_shape = _round_job(round_stacks, layer + 1, m // tm)
    outs = pl.pallas_call(
        functools.partial(_outproj_kernel, n_round=len(round_stacks)),
        grid=(m // tm,),
        in_specs=[pl.BlockSpec((tm, WIDTH_A), row), pl.BlockSpec((tm, WIDTH_B), row),
                  pl.BlockSpec((tm, D_MODEL), row),
                  _resident_weight(w),
                  _layer_vec(WIDTH_A, layer), _layer_vec(WIDTH_B, layer),
                  _layer_vec(D_MODEL, layer), _layer_vec(D_MODEL, layer)] + round_in,
        out_specs=[pl.BlockSpec((tm, D_MODEL), row)] + round_out,
        out_shape=[jax.ShapeDtypeStruct((m, D_MODEL), F32)] + round_shape,
        compiler_params=_params("arbitrary"),
        name="outproj",
    )(ya, yb, x, w, gna, gnb, g, b, *round_stacks)
    return outs[0], outs[1:]


def _memkv_kernel(mem_ref, w_ref, kv_ref):
    kv_ref[...] = _dot(mem_ref[...].astype(BF16), w_ref[...].astype(BF16)).astype(BF16)


def _memkv(mem, w_mkv, tn=1024):
    rows = mem.shape[0]
    return pl.pallas_call(
        _memkv_kernel,
        grid=(DEPTH, 2 * D_MODEL // tn),
        in_specs=[pl.BlockSpec((rows, D_MODEL), lambda i, j: (0, 0)),
                  pl.BlockSpec((None, D_MODEL, tn), lambda i, j: (i, 0, j))],
        out_specs=pl.BlockSpec((None, rows, tn), lambda i, j: (i, 0, j)),
        out_shape=jax.ShapeDtypeStruct((DEPTH, rows, 2 * D_MODEL), BF16),
        compiler_params=_params("arbitrary", "arbitrary"),
        name="memkv",
    )(mem, w_mkv)


def _memattn_kernel(x_ref, wq_ref, k_ref, v_ref, wo_ref, g_ref, b_ref, out_ref):
    x = x_ref[...]
    q = (_dot(x.astype(BF16), wq_ref[...]) * MEM_HEAD_DIM ** -0.5).astype(BF16)
    heads = []
    for h in range(N_MEM_HEADS):
        cols = slice(h * MEM_HEAD_DIM, (h + 1) * MEM_HEAD_DIM)
        s = _dot_nt(q[:, cols], k_ref[:, cols])
        m = jnp.max(s, axis=-1, keepdims=True)
        p = jnp.exp(s - m)
        l = jnp.sum(p, axis=-1, keepdims=True)
        o = _dot(p.astype(BF16), v_ref[:, cols])
        heads.append((o * (1.0 / l)).astype(BF16))
    y = _dot(jnp.concatenate(heads, axis=1), wo_ref[...])
    out_ref[...] = _layernorm(ALPHA * x + y, g_ref[...], b_ref[...])


def _memattn(x, wq, kv, wo, g, b, batch, seq, layer, tm=512):
    n_mem = kv.shape[1] // batch
    per_batch = seq // tm
    row = lambda i: (i, 0)
    weight = _resident_weight(wq)
    kv_spec = lambda half: pl.BlockSpec((None, n_mem, D_MODEL),
                                        lambda i: (layer, i // per_batch, half))
    return pl.pallas_call(
        _memattn_kernel,
        grid=(batch * seq // tm,),
        in_specs=[pl.BlockSpec((tm, D_MODEL), row), weight, kv_spec(0), kv_spec(1), weight,
                  _layer_vec(D_MODEL, layer), _layer_vec(D_MODEL, layer)],
        out_specs=pl.BlockSpec((tm, D_MODEL), row),
        out_shape=jax.ShapeDtypeStruct((batch * seq, D_MODEL), F32),
        compiler_params=_params("arbitrary"),
        name="memattn",
    )(x, wq, kv, kv, wo, g, b)


def _mlp_kernel(x_ref, wu_ref, wd_ref, g_ref, b_ref, out_ref, xb_ref):
    j = pl.program_id(1)

    @pl.when(j == 0)
    def _():
        xb_ref[...] = x_ref[...].astype(BF16)
        out_ref[...] = jnp.zeros_like(out_ref)

    h = jnp.maximum(_dot(xb_ref[...], wu_ref[...].astype(BF16)), 0.0)
    out_ref[...] += _dot((h * h).astype(BF16), wd_ref[...].astype(BF16))

    @pl.when(j == pl.num_programs(1) - 1)
    def _():
        out_ref[...] = _layernorm(ALPHA * x_ref[...] + out_ref[...], g_ref[...], b_ref[...])


def _mlp(x, wu, wd, g, b, layer, tm=1024, tf=512):
    m = x.shape[0]
    row = lambda i, j: (i, 0)
    return pl.pallas_call(
        _mlp_kernel,
        grid=(m // tm, D_FF // tf),
        in_specs=[_single_buffered((tm, D_MODEL), row),
                  pl.BlockSpec((None, D_MODEL, tf), lambda i, j: (layer, 0, j)),
                  pl.BlockSpec((None, tf, D_MODEL), lambda i, j: (layer, j, 0)),
                  _layer_vec(D_MODEL, layer), _layer_vec(D_MODEL, layer)],
        out_specs=pl.BlockSpec((tm, D_MODEL), row),
        out_shape=jax.ShapeDtypeStruct((m, D_MODEL), F32),
        scratch_shapes=[pltpu.VMEM((tm, D_MODEL), BF16)],
        compiler_params=_params("arbitrary", "arbitrary"),
        name="mlp",
    )(x, wu, wd, g, b)


def _rope_lane_tables(positions, head_dim, rot_dim):
    half = rot_dim // 2
    inv_freq = ROPE_THETA ** (-jnp.arange(0, rot_dim, 2, dtype=F32) / rot_dim)
    ang = positions.astype(F32).reshape(-1, 1) * inv_freq
    cos, sin = jnp.cos(ang), jnp.sin(ang)
    rows = ang.shape[0]
    pad = jnp.zeros((rows, head_dim - rot_dim), F32)
    zero = jnp.zeros((rows, half), F32)
    reps = BLOCK // head_dim
    c = jnp.tile(jnp.concatenate([cos, cos, pad + 1.0], axis=1), (1, reps))
    s_up = jnp.tile(jnp.concatenate([zero, sin, pad], axis=1), (1, reps))
    s_dn = jnp.tile(jnp.concatenate([-sin, zero, pad], axis=1), (1, reps))
    return c, s_up, s_dn


def kernel(x, mem, positions, w_in, gn_a, gn_b, sinks, w_out, ln_mix_g, ln_mix_b, w_mq, w_mkv,
           w_mo, ln_mem_g, ln_mem_b, w_up, w_down, ln_ff_g, ln_ff_b):
    batch, seq, _ = x.shape
    tabs_a = _rope_lane_tables(positions, HEAD_DIM_A, ROT_DIM_A)
    tabs_b = _rope_lane_tables(positions, HEAD_DIM_B, ROT_DIM_B)
    w_in_i, w_out_i, w_mq_i, w_mo_i = _round_layer((w_in, w_out, w_mq, w_mo), 0)
    gn_a, gn_b, ln_mix_g, ln_mix_b, ln_mem_g, ln_mem_b, ln_ff_g, ln_ff_b = (
        v.reshape(DEPTH, 1, -1)
        for v in (gn_a, gn_b, ln_mix_g, ln_mix_b, ln_mem_g, ln_mem_b, ln_ff_g, ln_ff_b))
    kv = _memkv(mem.reshape(-1, D_MODEL), w_mkv)
    h = x.reshape(batch * seq, D_MODEL)
    for i in range(DEPTH):
        more = i + 1 < DEPTH
        (qa, ka, va, qb, kvb), next_in = _inproj(
            h, w_in_i, tabs_a, tabs_b, (w_in,) if more else (), i + 1)
        ya = _attn_a(qa, ka, va, batch, seq)
        yb = _attn_b(sinks[i], qb, kvb, batch, seq)
        h, next_rest = _outproj(ya, yb, h, w_out_i, gn_a, gn_b, ln_mix_g, ln_mix_b, i,
                                (w_out, w_mq, w_mo) if more else ())
        h = _memattn(h, w_mq_i, kv, w_mo_i, ln_mem_g, ln_mem_b, batch, seq, i)
        h = _mlp(h, w_up, w_down, ln_ff_g, ln_ff_b, i)
        if more:
            (w_in_i,), (w_out_i, w_mq_i, w_mo_i) = next_in, next_rest
    return h.reshape(batch, seq, D_MODEL)
```

```python
import functools
import math

import jax
import jax.numpy as jnp
from jax import lax
from jax.experimental import pallas as pl
from jax.experimental.pallas import tpu as pltpu

D_MODEL = 2048
DEPTH = 4
N_MEM_HEADS = 4
MEM_HEAD_DIM = D_MODEL // N_MEM_HEADS
HEAD_DIM_A = 128
N_HEADS_A = 8
WIDTH_A = N_HEADS_A * HEAD_DIM_A
HEAD_DIM_B = 64
N_Q_HEADS_B = 16
N_KV_HEADS_B = 2
WIDTH_B = N_Q_HEADS_B * HEAD_DIM_B
KV_WIDTH_B = N_KV_HEADS_B * HEAD_DIM_B
ROPE_THETA = 500000.0
ROT_DIM_A = HEAD_DIM_A // 4
ROT_DIM_B = HEAD_DIM_B // 4
D_FF = 4 * D_MODEL
BLOCK = 128
ALPHA = (2 * DEPTH) ** 0.25
LN_EPS = 1e-5
RMS_EPS = 1e-6
NEG_INF = -1e30
LOG2E = math.log2(math.e)

PIECE = 16
GROUP = PIECE * PIECE
UNROLL = 4

BF16 = jnp.bfloat16
F32 = jnp.float32

VMEM_LIMIT_BYTES = 58 * 1024 * 1024


def _params(*semantics):
    return pltpu.CompilerParams(dimension_semantics=semantics,
                                vmem_limit_bytes=VMEM_LIMIT_BYTES)


def _single_buffered(block_shape, index_map):
    return pl.BlockSpec(block_shape, index_map, pipeline_mode=pl.Buffered(1))


def _resident_weight(w):
    return _single_buffered((None,) + w.shape[1:], lambda *_: (0, 0, 0))


def _round_job(stacks, layer, steps):
    rows = [w.shape[1] // steps for w in stacks]
    assert all(r * steps == w.shape[1] and r % PIECE == 0 for r, w in zip(rows, stacks))
    blocks = [(None, r, w.shape[2]) for r, w in zip(rows, stacks)]
    return ([pl.BlockSpec(blk, lambda i: (layer, i, 0)) for blk in blocks],
            [pl.BlockSpec(blk, lambda i: (0, i, 0)) for blk in blocks],
            [jax.ShapeDtypeStruct((1,) + w.shape[1:], BF16) for w in stacks])


def _round_slices(src_refs, dst_refs):
    for src, dst in zip(src_refs, dst_refs):
        dst[...] = src[...].astype(BF16)


def _round_kernel(*refs):
    _round_slices(refs[:len(refs) // 2], refs[len(refs) // 2:])


def _round_layer(stacks, layer, steps=8):
    in_specs, out_specs, out_shape = _round_job(stacks, layer, steps)
    return pl.pallas_call(
        _round_kernel, grid=(steps,), in_specs=in_specs, out_specs=out_specs,
        out_shape=out_shape, compiler_params=_params("arbitrary"), name="round_weights",
    )(*stacks)


def _layernorm(z, g, b):
    mu = jnp.mean(z, axis=-1, keepdims=True)
    zc = z - mu
    var = jnp.mean(zc * zc, axis=-1, keepdims=True)
    return zc * lax.rsqrt(var + LN_EPS) * g + b


def _dot_nt(a, b):
    return lax.dot_general(a, b, (((1,), (1,)), ((), ())), preferred_element_type=F32)


def _dot(a, b):
    return jnp.dot(a, b, preferred_element_type=F32)


def _rope(t, c, s_up, s_dn, half):
    return (t * c + pltpu.roll(t, half, 1) * s_up
            + pltpu.roll(t, BLOCK - half, 1) * s_dn)


def _inproj_kernel(*refs, n_round):
    x_ref, w_ref, ca_ref, ua_ref, da_ref, cb_ref, ub_ref, db_ref = refs[:8]
    qa_ref, ka_ref, va_ref, qb_ref, kvb_ref = refs[8 + n_round:13 + n_round]
    _round_slices(refs[8:8 + n_round], refs[13 + n_round:])
    x = x_ref[...].astype(BF16)
    ca, ua, da = ca_ref[...], ua_ref[...], da_ref[...]
    cb, ub, db = cb_ref[...], ub_ref[...], db_ref[...]
    scale_a = HEAD_DIM_A ** -0.5 * LOG2E
    scale_b = HEAD_DIM_B ** -0.5 * LOG2E
    chunk = 4 * BLOCK

    def proj(col, width):
        return _dot(x, w_ref[:, col:col + width])

    for c0 in range(0, WIDTH_A, chunk):
        hq = proj(c0, chunk)
        hk = proj(WIDTH_A + c0, chunk)
        for j in range(0, chunk, BLOCK):
            sl = slice(c0 + j, c0 + j + BLOCK)
            qa_ref[:, sl] = (_rope(hq[:, j:j + BLOCK], ca, ua, da, ROT_DIM_A // 2)
                             * scale_a).astype(BF16)
            ka_ref[:, sl] = _rope(hk[:, j:j + BLOCK], ca, ua, da,
                                  ROT_DIM_A // 2).astype(BF16)
        va_ref[:, c0:c0 + chunk] = proj(2 * WIDTH_A + c0, chunk).astype(BF16)
    for c0 in range(0, WIDTH_B, chunk):
        hq = proj(3 * WIDTH_A + c0, chunk)
        for j in range(0, chunk, BLOCK):
            qb_ref[:, c0 + j:c0 + j + BLOCK] = (
                _rope(hq[:, j:j + BLOCK], cb, ub, db, ROT_DIM_B // 2) * scale_b).astype(BF16)
    hkv = proj(3 * WIDTH_A + WIDTH_B, 2 * KV_WIDTH_B)
    kvb_ref[:, :KV_WIDTH_B] = _rope(hkv[:, :KV_WIDTH_B], cb, ub, db,
                                    ROT_DIM_B // 2).astype(BF16)
    kvb_ref[:, KV_WIDTH_B:] = hkv[:, KV_WIDTH_B:].astype(BF16)


def _inproj(x, w, tabs_a, tabs_b, round_stacks=(), round_layer=0, tm=512):
    m = x.shape[0]
    row = lambda i: (i, 0)
    tab_spec = pl.BlockSpec((tm, BLOCK), row)
    out = lambda width: jax.ShapeDtypeStruct((m, width), BF16)
    round_in, round_out, round_shape = _round_job(round_stacks, round_layer, m // tm)
    outs = pl.pallas_call(
        functools.partial(_inproj_kernel, n_round=len(round_stacks)),
        grid=(m // tm,),
        in_specs=[pl.BlockSpec((tm, D_MODEL), row), _resident_weight(w)] + [tab_spec] * 6
        + round_in,
        out_specs=[pl.BlockSpec((tm, WIDTH_A), row)] * 3
        + [pl.BlockSpec((tm, WIDTH_B), row), pl.BlockSpec((tm, 2 * KV_WIDTH_B), row)]
        + round_out,
        out_shape=[out(WIDTH_A)] * 3 + [out(WIDTH_B), out(2 * KV_WIDTH_B)] + round_shape,
        compiler_params=_params("arbitrary"),
        name="inproj",
    )(x, w, *tabs_a, *tabs_b, *round_stacks)
    return outs[:5], outs[5:]


def _band_mask(max_dist):
    qi = lax.broadcasted_iota(jnp.int32, (BLOCK, 2 * BLOCK), 0) + BLOCK
    kj = lax.broadcasted_iota(jnp.int32, (BLOCK, 2 * BLOCK), 1)
    dist = qi - kj
    return (dist >= 0) & (dist <= max_dist)


def _softmax_block(q, k, v, bias):
    s = _dot_nt(q, k) + bias
    m = jnp.max(s, axis=-1, keepdims=True)
    p = jnp.exp2(s - m)
    l = jnp.sum(p, axis=-1, keepdims=True)
    return m, l, _dot(p.astype(BF16), v)


def _softmax_block_onto(q, k, v, bias, m_prev, l_prev, acc_prev):
    s = _dot_nt(q, k) + bias
    m = jnp.maximum(m_prev, jnp.max(s, axis=-1, keepdims=True))
    scale_prev = jnp.exp2(m_prev - m)
    p = jnp.exp2(s - jnp.concatenate([m] * (s.shape[1] // BLOCK), axis=1))
    l = scale_prev * l_prev + jnp.sum(p, axis=-1, keepdims=True)
    return m, l, scale_prev * acc_prev + _dot(p.astype(BF16), v)


def _gather(ref, offsets, cols):
    return jnp.concatenate(
        [ref[pl.ds(pl.multiple_of(o, PIECE), PIECE), cols] for o in offsets], axis=0)


def _scatter(ref, offsets, cols, val):
    for i, o in enumerate(offsets):
        ref[pl.ds(pl.multiple_of(o, PIECE), PIECE), cols] = val[i * PIECE:(i + 1) * PIECE]


def _attn_a_kernel(perm_ref, q_ref, k_ref, v_ref, o_ref,
                   qt_ref, kt_ref, vt_ref, acc4_ref, m4_ref, l4_ref, acc16_ref, m16_ref, l16_ref,
                   bias_ref, *, heads):
    seq = q_ref.shape[0]
    n_groups = seq // GROUP
    perm = perm_ref[...]
    head_cols = [slice(h * HEAD_DIM_A, (h + 1) * HEAD_DIM_A) for h in range(heads)]
    all_lanes = slice(0, BLOCK)
    tile = lambda col: jnp.broadcast_to(col, (BLOCK, BLOCK))

    def i4_of(idx):
        return 64 * (idx >> 6) + 4 * (idx & 15) + ((idx >> 4) & 3)

    qi = lax.broadcasted_iota(jnp.int32, (BLOCK, 2 * BLOCK), 0)
    kj = lax.broadcasted_iota(jnp.int32, (BLOCK, 2 * BLOCK), 1)
    dist4 = BLOCK + i4_of(qi) - i4_of(kj)
    for i, band in enumerate((_band_mask(BLOCK), (dist4 >= 0) & (dist4 <= BLOCK))):
        bias_ref[i] = jnp.where(band, 0.0, NEG_INF)
        bias_ref[2 + i] = jnp.where(band & (kj >= BLOCK), 0.0, NEG_INF)

    def transpose_groups(t, carry):
        for i in range(2):
            rows = pl.ds(pl.multiple_of((2 * t + i) * GROUP, GROUP), GROUP)
            for src, dst in ((q_ref, qt_ref), (k_ref, kt_ref), (v_ref, vt_ref)):
                dst[rows, :] = _dot(perm, src[rows, :]).astype(BF16)
        return carry

    lax.fori_loop(0, n_groups // 2, transpose_groups, 0)

    def dilation4(r4, carry):
        piece = [PIECE * (r4 + 4 * k) for k in range(4)]
        for n in range(seq // 4 // BLOCK):
            q_off = [GROUP * (2 * n + g) + p for g in range(2) for p in piece]
            if n == 0:
                k_off, bias = q_off + q_off, bias_ref[3]
            else:
                k_off, bias = [o - 2 * GROUP for o in q_off] + q_off, bias_ref[1]
            for h, cols in enumerate(head_cols):
                m, l, acc = _softmax_block(_gather(qt_ref, q_off, cols),
                                           _gather(kt_ref, k_off, cols),
                                           _gather(vt_ref, k_off, cols), bias)
                _scatter(acc4_ref, q_off, cols, acc)
                _scatter(m4_ref.at[h], q_off, all_lanes, tile(m))
                _scatter(l4_ref.at[h], q_off, all_lanes, tile(l))
        return carry

    lax.fori_loop(0, 4, dilation4, 0)

    groups_per_block = BLOCK // PIECE

    def dilation16(t, carry):
        for r, n in [(UNROLL * t + i, n) for i in range(UNROLL)
                     for n in range(seq // PIECE // BLOCK)]:
            q_off = [GROUP * (groups_per_block * n + g) + PIECE * r
                     for g in range(groups_per_block)]
            if n == 0:
                k_off, bias = q_off + q_off, bias_ref[2]
            else:
                k_off = [o - GROUP * groups_per_block for o in q_off] + q_off
                bias = bias_ref[0]
            for h, cols in enumerate(head_cols):
                m, l, acc = _softmax_block(_gather(qt_ref, q_off, cols),
                                           _gather(kt_ref, k_off, cols),
                                           _gather(vt_ref, k_off, cols), bias)
                _scatter(acc16_ref, q_off, cols, acc)
                _scatter(m16_ref.at[h], q_off, all_lanes, tile(m))
                _scatter(l16_ref.at[h], q_off, all_lanes, tile(l))
        return carry

    lax.fori_loop(0, PIECE // UNROLL, dilation16, 0)

    def dilation1(g, first):
        g = jnp.int32(g)
        rows = pl.ds(pl.multiple_of(g * GROUP, GROUP), GROUP)
        accs, stats = [], []
        for h, cols in enumerate(head_cols):
            m4, m16 = m4_ref[h, rows, :], m16_ref[h, rows, :]
            m = jnp.maximum(m4, m16).astype(BF16)
            w4, w16 = jnp.exp2(m4 - m.astype(F32)), jnp.exp2(m16 - m.astype(F32))
            l = w4 * l4_ref[h, rows, :] + w16 * l16_ref[h, rows, :]
            l_hi = l.astype(BF16)
            accs.append((w4 * acc4_ref[rows, cols] + w16 * acc16_ref[rows, cols]).astype(BF16))
            stats += [m, l_hi, (l - l_hi.astype(F32)).astype(BF16)]
        carry = _dot(perm, jnp.concatenate(accs + stats, axis=1))
        for j in range(GROUP // BLOCK):
            local = slice(j * BLOCK, (j + 1) * BLOCK)
            r0 = pl.multiple_of(g * GROUP + j * BLOCK, BLOCK)
            for h, cols in enumerate(head_cols):
                stats = heads * HEAD_DIM_A + 3 * BLOCK * h
                m_prev = carry[local, stats:stats + BLOCK]
                l_prev = (carry[local, stats + BLOCK:stats + 2 * BLOCK]
                          + carry[local, stats + 2 * BLOCK:stats + 3 * BLOCK])
                if first and j == 0:
                    p0, bias = (r0, r0), bias_ref[2]
                else:
                    p0 = pl.multiple_of(g * GROUP + (j - 1) * BLOCK, BLOCK)
                    p0, bias = (p0, r0), bias_ref[0]
                k2 = jnp.concatenate([k_ref[pl.ds(p, BLOCK), cols] for p in p0], axis=0)
                v2 = jnp.concatenate([v_ref[pl.ds(p, BLOCK), cols] for p in p0], axis=0)
                _, l, acc = _softmax_block_onto(q_ref[pl.ds(r0, BLOCK), cols], k2, v2, bias,
                                                m_prev, l_prev, carry[local, cols])
                o_ref[pl.ds(r0, BLOCK), cols] = (acc * (1.0 / l)).astype(BF16)

    def later_groups(t, carry):
        for i in range(3):
            dilation1(1 + 3 * t + i, False)
        return carry

    dilation1(0, True)
    lax.fori_loop(0, (n_groups - 1) // 3, later_groups, 0)


def _group_transpose_matrix():
    idx = jnp.arange(GROUP)
    return (idx[:, None] == PIECE * (idx[None, :] % PIECE) + idx[None, :] // PIECE).astype(BF16)


def _attn_a(qa, ka, va, batch, seq, heads=2):
    width = heads * HEAD_DIM_A
    view = lambda t: t.reshape(batch, seq, WIDTH_A)
    blk = pl.BlockSpec((None, seq, width), lambda b, h: (b, 0, h))
    stat = pltpu.VMEM((heads, seq, BLOCK), F32)
    o = pl.pallas_call(
        functools.partial(_attn_a_kernel, heads=heads),
        grid=(batch, N_HEADS_A // heads),
        in_specs=[pl.BlockSpec((GROUP, GROUP), lambda b, h: (0, 0)), blk, blk, blk],
        out_specs=blk,
        out_shape=jax.ShapeDtypeStruct((batch, seq, WIDTH_A), BF16),
        scratch_shapes=[pltpu.VMEM((seq, width), BF16)] * 3
        + [pltpu.VMEM((seq, width), F32), stat, stat] * 2
        + [pltpu.VMEM((4, BLOCK, 2 * BLOCK), F32)],
        compiler_params=_params("arbitrary", "arbitrary"),
        name="attn_a",
    )(_group_transpose_matrix(), view(qa), view(ka), view(va))
    return o.reshape(batch * seq, WIDTH_A)


def _attn_b_kernel(sink_ref, q_ref, kvp_ref, kv_ref, o_ref, *, rows):
    stack = lambda t: jnp.concatenate([t, t], axis=0)
    qi = lax.broadcasted_iota(jnp.int32, (BLOCK, BLOCK), 0)
    kj = lax.broadcasted_iota(jnp.int32, (BLOCK, BLOCK), 1)
    from_prev = stack(kj > qi)
    no_prev = jnp.where(pl.program_id(1) == 0, NEG_INF, 0.0)
    low = kj < HEAD_DIM_B
    low2 = stack(low)
    row2 = lax.broadcasted_iota(jnp.int32, (2 * BLOCK, 1), 0)
    heads_per_group = N_Q_HEADS_B // N_KV_HEADS_B
    for j in range(rows // BLOCK):
        r0 = j * BLOCK
        if j == 0:
            kv2 = jnp.concatenate([kvp_ref[...], kv_ref[:BLOCK, :]], axis=0)
        else:
            kv2 = kv_ref[r0 - BLOCK:r0 + BLOCK, :]
        kv2 = kv2.astype(F32)
        k2, v2 = kv2[:, :KV_WIDTH_B], kv2[:, KV_WIDTH_B:]
        k2r, v2r = pltpu.roll(k2, HEAD_DIM_B, 1), pltpu.roll(v2, HEAD_DIM_B, 1)
        kdup = [jnp.where(low2, k2, k2r).astype(BF16), jnp.where(low2, k2r, k2).astype(BF16)]
        vdup = [jnp.where(low2, v2, v2r).astype(BF16), jnp.where(low2, v2r, v2).astype(BF16)]
        for t in range(WIDTH_B // BLOCK):
            g = (2 * t) // heads_per_group
            cols = slice(t * BLOCK, (t + 1) * BLOCK)
            q = q_ref[r0:r0 + BLOCK, cols]
            zero = jnp.zeros_like(q)
            q2 = jnp.concatenate([jnp.where(low, q, zero), jnp.where(low, zero, q)], axis=0)
            s2 = _dot_nt(q2, kdup[g])
            s_prev = s2[:, :BLOCK] + no_prev if j == 0 else s2[:, :BLOCK]
            s = jnp.where(from_prev, s_prev, s2[:, BLOCK:])
            m = jnp.max(s, axis=-1, keepdims=True)
            p = jnp.exp2(s - m)
            l = jnp.sum(p, axis=-1, keepdims=True)
            p2 = jnp.concatenate([jnp.where(from_prev, p, 0.0), jnp.where(from_prev, 0.0, p)],
                                 axis=1)
            acc = _dot(p2.astype(BF16), vdup[g])
            sink = jnp.where(row2 < BLOCK, sink_ref[2 * t], sink_ref[2 * t + 1]) * LOG2E
            m2 = jnp.maximum(m, sink)
            c = jnp.exp2(m - m2)
            den = l * c + jnp.exp2(sink - m2)
            y = acc * (c / den)
            o_ref[r0:r0 + BLOCK, cols] = jnp.where(low, y[:BLOCK], y[BLOCK:]).astype(BF16)


def _attn_b(sinks, qb, kvb, batch, seq, rows=512):
    nblk = seq // rows
    prev_per_blk = rows // BLOCK
    view = lambda t: t.reshape(batch, seq, t.shape[-1])
    cur = lambda b, n: (b, n, 0)
    prev = lambda b, n: (b, jnp.maximum(n * prev_per_blk - 1, 0), 0)
    o = pl.pallas_call(
        functools.partial(_attn_b_kernel, rows=rows),
        grid=(batch, nblk),
        in_specs=[pl.BlockSpec(memory_space=pltpu.SMEM),
                  pl.BlockSpec((None, rows, WIDTH_B), cur),
                  pl.BlockSpec((None, BLOCK, 2 * KV_WIDTH_B), prev),
                  pl.BlockSpec((None, rows, 2 * KV_WIDTH_B), cur)],
        out_specs=pl.BlockSpec((None, rows, WIDTH_B), cur),
        out_shape=jax.ShapeDtypeStruct((batch, seq, WIDTH_B), BF16),
        compiler_params=_params("arbitrary", "arbitrary"),
        name="attn_b",
    )(sinks, view(qb), view(kvb), view(kvb))
    return o.reshape(batch * seq, WIDTH_B)


def _rms_gain(y, gain):
    return y * lax.rsqrt(jnp.mean(y * y, axis=-1, keepdims=True) + RMS_EPS) * gain


def _outproj_kernel(*refs, n_round):
    ya_ref, yb_ref, x_ref, w_ref, gna_ref, gnb_ref, g_ref, b_ref = refs[:8]
    out_ref = refs[8 + n_round]
    _round_slices(refs[8:8 + n_round], refs[9 + n_round:])
    ya = _rms_gain(ya_ref[...].astype(F32), gna_ref[...]).astype(BF16)
    yb = _rms_gain(yb_ref[...].astype(F32), gnb_ref[...]).astype(BF16)
    y = _dot(jnp.concatenate([ya, yb], axis=1), w_ref[...])
    out_ref[...] = _layernorm(ALPHA * x_ref[...] + y, g_ref[...], b_ref[...])


def _layer_vec(width, layer):
    return pl.BlockSpec((None, 1, width), lambda *_: (layer, 0, 0))


def _outproj(ya, yb, x, w, gna, gnb, g, b, layer, round_stacks=(), tm=512):
    m = x.shape[0]
    row = lambda i: (i, 0)
    round_in, round_out, round_shape = _round_job(round_stacks, layer + 1, m // tm)
    outs = pl.pallas_call(
        functools.partial(_outproj_kernel, n_round=len(round_stacks)),
        grid=(m // tm,),
        in_specs=[pl.BlockSpec((tm, WIDTH_A), row), pl.BlockSpec((tm, WIDTH_B), row),
                  pl.BlockSpec((tm, D_MODEL), row),
                  _resident_weight(w),
                  _layer_vec(WIDTH_A, layer), _layer_vec(WIDTH_B, layer),
                  _layer_vec(D_MODEL, layer), _layer_vec(D_MODEL, layer)] + round_in,
        out_specs=[pl.BlockSpec((tm, D_MODEL), row)] + round_out,
        out_shape=[jax.ShapeDtypeStruct((m, D_MODEL), F32)] + round_shape,
        compiler_params=_params("arbitrary"),
        name="outproj",
    )(ya, yb, x, w, gna, gnb, g, b, *round_stacks)
    return outs[0], outs[1:]


def _memkv_kernel(mem_ref, w_ref, kv_ref):
    kv_ref[...] = _dot(mem_ref[...].astype(BF16), w_ref[...].astype(BF16)).astype(BF16)


def _memkv(mem, w_mkv, tn=1024):
    rows = mem.shape[0]
    return pl.pallas_call(
        _memkv_kernel,
        grid=(DEPTH, 2 * D_MODEL // tn),
        in_specs=[pl.BlockSpec((rows, D_MODEL), lambda i, j: (0, 0)),
                  pl.BlockSpec((None, D_MODEL, tn), lambda i, j: (i, 0, j))],
        out_specs=pl.BlockSpec((None, rows, tn), lambda i, j: (i, 0, j)),
        out_shape=jax.ShapeDtypeStruct((DEPTH, rows, 2 * D_MODEL), BF16),
        compiler_params=_params("arbitrary", "arbitrary"),
        name="memkv",
    )(mem, w_mkv)


def _memattn_kernel(x_ref, wq_ref, k_ref, v_ref, wo_ref, g_ref, b_ref, out_ref):
    x = x_ref[...]
    q = (_dot(x.astype(BF16), wq_ref[...]) * MEM_HEAD_DIM ** -0.5).astype(BF16)
    heads = []
    for h in range(N_MEM_HEADS):
        cols = slice(h * MEM_HEAD_DIM, (h + 1) * MEM_HEAD_DIM)
        s = _dot_nt(q[:, cols], k_ref[:, cols])
        m = jnp.max(s, axis=-1, keepdims=True)
        p = jnp.exp(s - m)
        l = jnp.sum(p, axis=-1, keepdims=True)
        o = _dot(p.astype(BF16), v_ref[:, cols])
        heads.append((o * (1.0 / l)).astype(BF16))
    y = _dot(jnp.concatenate(heads, axis=1), wo_ref[...])
    out_ref[...] = _layernorm(ALPHA * x + y, g_ref[...], b_ref[...])


def _memattn(x, wq, kv, wo, g, b, batch, seq, layer, tm=512):
    n_mem = kv.shape[1] // batch
    per_batch = seq // tm
    row = lambda i: (i, 0)
    weight = _resident_weight(wq)
    kv_spec = lambda half: pl.BlockSpec((None, n_mem, D_MODEL),
                                        lambda i: (layer, i // per_batch, half))
    return pl.pallas_call(
        _memattn_kernel,
        grid=(batch * seq // tm,),
        in_specs=[pl.BlockSpec((tm, D_MODEL), row), weight, kv_spec(0), kv_spec(1), weight,
                  _layer_vec(D_MODEL, layer), _layer_vec(D_MODEL, layer)],
        out_specs=pl.BlockSpec((tm, D_MODEL), row),
        out_shape=jax.ShapeDtypeStruct((batch * seq, D_MODEL), F32),
        compiler_params=_params("arbitrary"),
        name="memattn",
    )(x, wq, kv, kv, wo, g, b)


def _mlp_kernel(x_ref, wu_ref, wd_ref, g_ref, b_ref, out_ref):
    j = pl.program_id(1)

    @pl.when(j == 0)
    def _():
        out_ref[...] = ALPHA * x_ref[...]

    h = jnp.maximum(_dot(x_ref[...].astype(BF16), wu_ref[...].astype(BF16)), 0.0)
    out_ref[...] += _dot((h * h).astype(BF16), wd_ref[...].astype(BF16))

    @pl.when(j == pl.num_programs(1) - 1)
    def _():
        out_ref[...] = _layernorm(out_ref[...], g_ref[...], b_ref[...])


def _mlp(x, wu, wd, g, b, layer, tm=1024, tf=512):
    m = x.shape[0]
    row = lambda i, j: (i, 0)
    return pl.pallas_call(
        _mlp_kernel,
        grid=(m // tm, D_FF // tf),
        in_specs=[pl.BlockSpec((tm, D_MODEL), row),
                  pl.BlockSpec((None, D_MODEL, tf), lambda i, j: (layer, 0, j)),
                  pl.BlockSpec((None, tf, D_MODEL), lambda i, j: (layer, j, 0)),
                  _layer_vec(D_MODEL, layer), _layer_vec(D_MODEL, layer)],
        out_specs=pl.BlockSpec((tm, D_MODEL), row),
        out_shape=jax.ShapeDtypeStruct((m, D_MODEL), F32),
        compiler_params=_params("arbitrary", "arbitrary"),
        name="mlp",
    )(x, wu, wd, g, b)


def _rope_lane_tables(positions, head_dim, rot_dim):
    half = rot_dim // 2
    inv_freq = ROPE_THETA ** (-jnp.arange(0, rot_dim, 2, dtype=F32) / rot_dim)
    ang = positions.astype(F32).reshape(-1, 1) * inv_freq
    cos, sin = jnp.cos(ang), jnp.sin(ang)
    rows = ang.shape[0]
    pad = jnp.zeros((rows, head_dim - rot_dim), F32)
    zero = jnp.zeros((rows, half), F32)
    reps = BLOCK // head_dim
    c = jnp.tile(jnp.concatenate([cos, cos, pad + 1.0], axis=1), (1, reps))
    s_up = jnp.tile(jnp.concatenate([zero, sin, pad], axis=1), (1, reps))
    s_dn = jnp.tile(jnp.concatenate([-sin, zero, pad], axis=1), (1, reps))
    return c, s_up, s_dn


def kernel(x, mem, positions, w_in, gn_a, gn_b, sinks, w_out, ln_mix_g, ln_mix_b, w_mq, w_mkv,
           w_mo, ln_mem_g, ln_mem_b, w_up, w_down, ln_ff_g, ln_ff_b):
    batch, seq, _ = x.shape
    tabs_a = _rope_lane_tables(positions, HEAD_DIM_A, ROT_DIM_A)
    tabs_b = _rope_lane_tables(positions, HEAD_DIM_B, ROT_DIM_B)
    w_in_i, w_out_i, w_mq_i, w_mo_i = _round_layer((w_in, w_out, w_mq, w_mo), 0)
    gn_a, gn_b, ln_mix_g, ln_mix_b, ln_mem_g, ln_mem_b, ln_ff_g, ln_ff_b = (
        v.reshape(DEPTH, 1, -1)
        for v in (gn_a, gn_b, ln_mix_g, ln_mix_b, ln_mem_g, ln_mem_b, ln_ff_g, ln_ff_b))
    kv = _memkv(mem.reshape(-1, D_MODEL), w_mkv)
    h = x.reshape(batch * seq, D_MODEL)
    for i in range(DEPTH):
        more = i + 1 < DEPTH
        (qa, ka, va, qb, kvb), next_in = _inproj(
            h, w_in_i, tabs_a, tabs_b, (w_in,) if more else (), i + 1)
        ya = _attn_a(qa, ka, va, batch, seq)
        yb = _attn_b(sinks[i], qb, kvb, batch, seq)
        h, next_rest = _outproj(ya, yb, h, w_out_i, gn_a, gn_b, ln_mix_g, ln_mix_b, i,
                                (w_out, w_mq, w_mo) if more else ())
        h = _memattn(h, w_mq_i, kv, w_mo_i, ln_mem_g, ln_mem_b, batch, seq, i)
        h = _mlp(h, w_up, w_down, ln_ff_g, ln_ff_b, i)
        if more:
            (w_in_i,), (w_out_i, w_mq_i, w_mo_i) = next_in, next_rest
    return h.reshape(batch, seq, D_MODEL)
```

```python
import functools
import math

import jax
import jax.numpy as jnp
from jax import lax
from jax.experimental import pallas as pl
from jax.experimental.pallas import tpu as pltpu

D_MODEL = 2048
DEPTH = 4
N_MEM_HEADS = 4
MEM_HEAD_DIM = D_MODEL // N_MEM_HEADS
HEAD_DIM_A = 128
N_HEADS_A = 8
WIDTH_A = N_HEADS_A * HEAD_DIM_A
HEAD_DIM_B = 64
N_Q_HEADS_B = 16
N_KV_HEADS_B = 2
WIDTH_B = N_Q_HEADS_B * HEAD_DIM_B
KV_WIDTH_B = N_KV_HEADS_B * HEAD_DIM_B
ROPE_THETA = 500000.0
ROT_DIM_A = HEAD_DIM_A // 4
ROT_DIM_B = HEAD_DIM_B // 4
D_FF = 4 * D_MODEL
BLOCK = 128
ALPHA = (2 * DEPTH) ** 0.25
LN_EPS = 1e-5
RMS_EPS = 1e-6
NEG_INF = -1e30
LOG2E = math.log2(math.e)

PIECE = 16
GROUP = PIECE * PIECE
GROUPS_PER_BODY = 5

BF16 = jnp.bfloat16
F32 = jnp.float32

VMEM_LIMIT_BYTES = 58 * 1024 * 1024


def _params(*semantics):
    return pltpu.CompilerParams(dimension_semantics=semantics,
                                vmem_limit_bytes=VMEM_LIMIT_BYTES)


def _single_buffered(block_shape, index_map):
    return pl.BlockSpec(block_shape, index_map, pipeline_mode=pl.Buffered(1))


def _resident_weight(w):
    return _single_buffered((None,) + w.shape[1:], lambda *_: (0, 0, 0))


def _round_job(stacks, layer, steps):
    rows = [w.shape[1] // steps for w in stacks]
    assert all(r * steps == w.shape[1] and r % PIECE == 0 for r, w in zip(rows, stacks))
    blocks = [(None, r, w.shape[2]) for r, w in zip(rows, stacks)]
    return ([pl.BlockSpec(blk, lambda i: (layer, i, 0)) for blk in blocks],
            [pl.BlockSpec(blk, lambda i: (0, i, 0)) for blk in blocks],
            [jax.ShapeDtypeStruct((1,) + w.shape[1:], BF16) for w in stacks])


def _round_slices(src_refs, dst_refs):
    for src, dst in zip(src_refs, dst_refs):
        dst[...] = src[...].astype(BF16)


def _round_kernel(*refs):
    _round_slices(refs[:len(refs) // 2], refs[len(refs) // 2:])


def _round_layer(stacks, layer, steps=8):
    in_specs, out_specs, out_shape = _round_job(stacks, layer, steps)
    return pl.pallas_call(
        _round_kernel, grid=(steps,), in_specs=in_specs, out_specs=out_specs,
        out_shape=out_shape, compiler_params=_params("arbitrary"), name="round_weights",
    )(*stacks)


def _layernorm(z, g, b):
    mu = jnp.mean(z, axis=-1, keepdims=True)
    zc = z - mu
    var = jnp.mean(zc * zc, axis=-1, keepdims=True)
    return zc * lax.rsqrt(var + LN_EPS) * g + b


def _dot_nt(a, b):
    return lax.dot_general(a, b, (((1,), (1,)), ((), ())), preferred_element_type=F32)


def _dot(a, b):
    return jnp.dot(a, b, preferred_element_type=F32)


def _rope(t, c, s_up, s_dn, half):
    return (t * c + pltpu.roll(t, half, 1) * s_up
            + pltpu.roll(t, BLOCK - half, 1) * s_dn)


def _inproj_kernel(*refs, n_round):
    x_ref, w_ref, ca_ref, ua_ref, da_ref, cb_ref, ub_ref, db_ref = refs[:8]
    qa_ref, ka_ref, va_ref, qb_ref, kvb_ref = refs[8 + n_round:13 + n_round]
    _round_slices(refs[8:8 + n_round], refs[13 + n_round:])
    x = x_ref[...].astype(BF16)
    ca, ua, da = ca_ref[...], ua_ref[...], da_ref[...]
    cb, ub, db = cb_ref[...], ub_ref[...], db_ref[...]
    scale_a = HEAD_DIM_A ** -0.5 * LOG2E
    scale_b = HEAD_DIM_B ** -0.5 * LOG2E
    chunk = 4 * BLOCK

    def proj(col, width):
        return _dot(x, w_ref[:, col:col + width])

    for c0 in range(0, WIDTH_A, chunk):
        hq = proj(c0, chunk)
        hk = proj(WIDTH_A + c0, chunk)
        for j in range(0, chunk, BLOCK):
            sl = slice(c0 + j, c0 + j + BLOCK)
            qa_ref[:, sl] = (_rope(hq[:, j:j + BLOCK], ca, ua, da, ROT_DIM_A // 2)
                             * scale_a).astype(BF16)
            ka_ref[:, sl] = _rope(hk[:, j:j + BLOCK], ca, ua, da,
                                  ROT_DIM_A // 2).astype(BF16)
        va_ref[:, c0:c0 + chunk] = proj(2 * WIDTH_A + c0, chunk).astype(BF16)
    for c0 in range(0, WIDTH_B, chunk):
        hq = proj(3 * WIDTH_A + c0, chunk)
        for j in range(0, chunk, BLOCK):
            qb_ref[:, c0 + j:c0 + j + BLOCK] = (
                _rope(hq[:, j:j + BLOCK], cb, ub, db, ROT_DIM_B // 2) * scale_b).astype(BF16)
    hkv = proj(3 * WIDTH_A + WIDTH_B, 2 * KV_WIDTH_B)
    kvb_ref[:, :KV_WIDTH_B] = _rope(hkv[:, :KV_WIDTH_B], cb, ub, db,
                                    ROT_DIM_B // 2).astype(BF16)
    kvb_ref[:, KV_WIDTH_B:] = hkv[:, KV_WIDTH_B:].astype(BF16)


def _inproj(x, w, tabs_a, tabs_b, round_stacks=(), round_layer=0, tm=512):
    m = x.shape[0]
    row = lambda i: (i, 0)
    tab_spec = pl.BlockSpec((tm, BLOCK), row)
    out = lambda width: jax.ShapeDtypeStruct((m, width), BF16)
    round_in, round_out, round_shape = _round_job(round_stacks, round_layer, m // tm)
    outs = pl.pallas_call(
        functools.partial(_inproj_kernel, n_round=len(round_stacks)),
        grid=(m // tm,),
        in_specs=[pl.BlockSpec((tm, D_MODEL), row), _resident_weight(w)] + [tab_spec] * 6
        + round_in,
        out_specs=[pl.BlockSpec((tm, WIDTH_A), row)] * 3
        + [pl.BlockSpec((tm, WIDTH_B), row), pl.BlockSpec((tm, 2 * KV_WIDTH_B), row)]
        + round_out,
        out_shape=[out(WIDTH_A)] * 3 + [out(WIDTH_B), out(2 * KV_WIDTH_B)] + round_shape,
        compiler_params=_params("arbitrary"),
        name="inproj",
    )(x, w, *tabs_a, *tabs_b, *round_stacks)
    return outs[:5], outs[5:]


def _band_mask(max_dist):
    qi = lax.broadcasted_iota(jnp.int32, (BLOCK, 2 * BLOCK), 0) + BLOCK
    kj = lax.broadcasted_iota(jnp.int32, (BLOCK, 2 * BLOCK), 1)
    dist = qi - kj
    return (dist >= 0) & (dist <= max_dist)


def _softmax_block(q, k, v, bias):
    s = _dot_nt(q, k) + bias
    m = jnp.max(s, axis=-1, keepdims=True)
    p = jnp.exp2(s - m)
    l = jnp.sum(p, axis=-1, keepdims=True)
    return m, l, _dot(p.astype(BF16), v)


def _softmax_block_onto(q, k, v, bias, m_prev, l_prev, acc_prev):
    s = _dot_nt(q, k) + bias
    m = jnp.maximum(m_prev, jnp.max(s, axis=-1, keepdims=True))
    scale_prev = jnp.exp2(m_prev - m)
    p = jnp.exp2(s - jnp.concatenate([m] * (s.shape[1] // BLOCK), axis=1))
    l = scale_prev * l_prev + jnp.sum(p, axis=-1, keepdims=True)
    return m, l, scale_prev * acc_prev + _dot(p.astype(BF16), v)


def _piece(offset):
    return pl.ds(offset if isinstance(offset, int) else pl.multiple_of(offset, PIECE), PIECE)


def _gather(ref, offsets, cols):
    return jnp.concatenate([ref[_piece(o), cols] for o in offsets], axis=0)


def _scatter(ref, offsets, cols, val):
    for i, o in enumerate(offsets):
        ref[_piece(o), cols] = val[i * PIECE:(i + 1) * PIECE]


def _attn_a_kernel(perm_ref, q_ref, k_ref, v_ref, o_ref,
                   qt_ref, kt_ref, vt_ref, acc4_ref, m4_ref, l4_ref, acc16_ref, m16_ref, l16_ref,
                   bias_ref, *, heads):
    seq = q_ref.shape[0]
    n_groups = seq // GROUP
    perm = perm_ref[...]
    head_cols = [slice(h * HEAD_DIM_A, (h + 1) * HEAD_DIM_A) for h in range(heads)]
    all_lanes = slice(0, BLOCK)
    tile = lambda col: jnp.broadcast_to(col, (BLOCK, BLOCK))

    def i4_of(idx):
        return 64 * (idx >> 6) + 4 * (idx & 15) + ((idx >> 4) & 3)

    qi = lax.broadcasted_iota(jnp.int32, (BLOCK, 2 * BLOCK), 0)
    kj = lax.broadcasted_iota(jnp.int32, (BLOCK, 2 * BLOCK), 1)
    dist4 = BLOCK + i4_of(qi) - i4_of(kj)
    for i, band in enumerate((_band_mask(BLOCK), (dist4 >= 0) & (dist4 <= BLOCK))):
        bias_ref[i] = jnp.where(band, 0.0, NEG_INF)
        bias_ref[2 + i] = jnp.where(band & (kj >= BLOCK), 0.0, NEG_INF)

    for g in range(n_groups):
        rows = slice(g * GROUP, (g + 1) * GROUP)
        for src, dst in ((q_ref, qt_ref), (k_ref, kt_ref), (v_ref, vt_ref)):
            dst[rows, :] = _dot(perm, src[rows, :]).astype(BF16)

    def transposed_block(q_off, prev_off, band, acc_ref, m_ref, l_ref):
        if prev_off is None:
            k_off, bias = q_off + q_off, bias_ref[2 + band]
        else:
            k_off, bias = prev_off + q_off, bias_ref[band]
        for h, cols in enumerate(head_cols):
            m, l, acc = _softmax_block(_gather(qt_ref, q_off, cols), _gather(kt_ref, k_off, cols),
                                       _gather(vt_ref, k_off, cols), bias)
            _scatter(acc_ref, q_off, cols, acc)
            _scatter(m_ref.at[h], q_off, all_lanes, tile(m))
            _scatter(l_ref.at[h], q_off, all_lanes, tile(l))

    for r4 in range(4):
        for n in range(seq // 4 // BLOCK):
            q_off = [GROUP * (2 * n + g) + PIECE * (r4 + 4 * k)
                     for g in range(2) for k in range(4)]
            prev_off = [o - 2 * GROUP for o in q_off] if n else None
            transposed_block(q_off, prev_off, 1, acc4_ref, m4_ref, l4_ref)

    groups_per_block = BLOCK // PIECE
    for r in range(PIECE):
        for n in range(seq // PIECE // BLOCK):
            q_off = [GROUP * (groups_per_block * n + g) + PIECE * r
                     for g in range(groups_per_block)]
            prev_off = [o - GROUP * groups_per_block for o in q_off] if n else None
            transposed_block(q_off, prev_off, 0, acc16_ref, m16_ref, l16_ref)

    def dilation1(g, first):
        g = jnp.int32(g)
        rows = pl.ds(pl.multiple_of(g * GROUP, GROUP), GROUP)
        accs, stats = [], []
        for h, cols in enumerate(head_cols):
            m4, m16 = m4_ref[h, rows, :], m16_ref[h, rows, :]
            m = jnp.maximum(m4, m16).astype(BF16)
            w4, w16 = jnp.exp2(m4 - m.astype(F32)), jnp.exp2(m16 - m.astype(F32))
            l = w4 * l4_ref[h, rows, :] + w16 * l16_ref[h, rows, :]
            l_hi = l.astype(BF16)
            accs.append((w4 * acc4_ref[rows, cols] + w16 * acc16_ref[rows, cols]).astype(BF16))
            stats += [m, l_hi, (l - l_hi.astype(F32)).astype(BF16)]
        carry = _dot(perm, jnp.concatenate(accs + stats, axis=1))
        for j in range(GROUP // BLOCK):
            local = slice(j * BLOCK, (j + 1) * BLOCK)
            r0 = pl.multiple_of(g * GROUP + j * BLOCK, BLOCK)
            for h, cols in enumerate(head_cols):
                stats = heads * HEAD_DIM_A + 3 * BLOCK * h
                m_prev = carry[local, stats:stats + BLOCK]
                l_prev = (carry[local, stats + BLOCK:stats + 2 * BLOCK]
                          + carry[local, stats + 2 * BLOCK:stats + 3 * BLOCK])
                if first and j == 0:
                    p0, bias = (r0, r0), bias_ref[2]
                else:
                    p0 = pl.multiple_of(g * GROUP + (j - 1) * BLOCK, BLOCK)
                    p0, bias = (p0, r0), bias_ref[0]
                k2 = jnp.concatenate([k_ref[pl.ds(p, BLOCK), cols] for p in p0], axis=0)
                v2 = jnp.concatenate([v_ref[pl.ds(p, BLOCK), cols] for p in p0], axis=0)
                _, l, acc = _softmax_block_onto(q_ref[pl.ds(r0, BLOCK), cols], k2, v2, bias,
                                                m_prev, l_prev, carry[local, cols])
                o_ref[pl.ds(r0, BLOCK), cols] = (acc * (1.0 / l)).astype(BF16)

    def later_groups(t, carry):
        for i in range(GROUPS_PER_BODY):
            dilation1(1 + GROUPS_PER_BODY * t + i, False)
        return carry

    dilation1(0, True)
    lax.fori_loop(0, (n_groups - 1) // GROUPS_PER_BODY, later_groups, 0)


def _group_transpose_matrix():
    idx = jnp.arange(GROUP)
    return (idx[:, None] == PIECE * (idx[None, :] % PIECE) + idx[None, :] // PIECE).astype(BF16)


def _attn_a(qa, ka, va, batch, seq, heads=2):
    assert (seq // GROUP - 1) % GROUPS_PER_BODY == 0
    width = heads * HEAD_DIM_A
    view = lambda t: t.reshape(batch, seq, WIDTH_A)
    blk = pl.BlockSpec((None, seq, width), lambda b, h: (b, 0, h))
    stat = pltpu.VMEM((heads, seq, BLOCK), F32)
    o = pl.pallas_call(
        functools.partial(_attn_a_kernel, heads=heads),
        grid=(batch, N_HEADS_A // heads),
        in_specs=[pl.BlockSpec((GROUP, GROUP), lambda b, h: (0, 0)), blk, blk, blk],
        out_specs=blk,
        out_shape=jax.ShapeDtypeStruct((batch, seq, WIDTH_A), BF16),
        scratch_shapes=[pltpu.VMEM((seq, width), BF16)] * 3
        + [pltpu.VMEM((seq, width), F32), stat, stat] * 2
        + [pltpu.VMEM((4, BLOCK, 2 * BLOCK), F32)],
        compiler_params=_params("arbitrary", "arbitrary"),
        name="attn_a",
    )(_group_transpose_matrix(), view(qa), view(ka), view(va))
    return o.reshape(batch * seq, WIDTH_A)


def _attn_b_kernel(sink_ref, q_ref, kvp_ref, kv_ref, o_ref, *, rows):
    stack = lambda t: jnp.concatenate([t, t], axis=0)
    qi = lax.broadcasted_iota(jnp.int32, (BLOCK, BLOCK), 0)
    kj = lax.broadcasted_iota(jnp.int32, (BLOCK, BLOCK), 1)
    from_prev = stack(kj > qi)
    no_prev = jnp.where(pl.program_id(1) == 0, NEG_INF, 0.0)
    low = kj < HEAD_DIM_B
    low2 = stack(low)
    row2 = lax.broadcasted_iota(jnp.int32, (2 * BLOCK, 1), 0)
    heads_per_group = N_Q_HEADS_B // N_KV_HEADS_B
    for j in range(rows // BLOCK):
        r0 = j * BLOCK
        if j == 0:
            kv2 = jnp.concatenate([kvp_ref[...], kv_ref[:BLOCK, :]], axis=0)
        else:
            kv2 = kv_ref[r0 - BLOCK:r0 + BLOCK, :]
        kv2 = kv2.astype(F32)
        k2, v2 = kv2[:, :KV_WIDTH_B], kv2[:, KV_WIDTH_B:]
        k2r, v2r = pltpu.roll(k2, HEAD_DIM_B, 1), pltpu.roll(v2, HEAD_DIM_B, 1)
        kdup = [jnp.where(low2, k2, k2r).astype(BF16), jnp.where(low2, k2r, k2).astype(BF16)]
        vdup = [jnp.where(low2, v2, v2r).astype(BF16), jnp.where(low2, v2r, v2).astype(BF16)]
        for t in range(WIDTH_B // BLOCK):
            g = (2 * t) // heads_per_group
            cols = slice(t * BLOCK, (t + 1) * BLOCK)
            q = q_ref[r0:r0 + BLOCK, cols]
            zero = jnp.zeros_like(q)
            q2 = jnp.concatenate([jnp.where(low, q, zero), jnp.where(low, zero, q)], axis=0)
            s2 = _dot_nt(q2, kdup[g])
            s_prev = s2[:, :BLOCK] + no_prev if j == 0 else s2[:, :BLOCK]
            s = jnp.where(from_prev, s_prev, s2[:, BLOCK:])
            m = jnp.max(s, axis=-1, keepdims=True)
            p = jnp.exp2(s - m)
            l = jnp.sum(p, axis=-1, keepdims=True)
            p2 = jnp.concatenate([jnp.where(from_prev, p, 0.0), jnp.where(from_prev, 0.0, p)],
                                 axis=1)
            acc = _dot(p2.astype(BF16), vdup[g])
            sink = jnp.where(row2 < BLOCK, sink_ref[2 * t], sink_ref[2 * t + 1]) * LOG2E
            m2 = jnp.maximum(m, sink)
            c = jnp.exp2(m - m2)
            den = l * c + jnp.exp2(sink - m2)
            y = acc * (c / den)
            o_ref[r0:r0 + BLOCK, cols] = jnp.where(low, y[:BLOCK], y[BLOCK:]).astype(BF16)


def _attn_b(sinks, qb, kvb, batch, seq, rows=1024):
    nblk = seq // rows
    prev_per_blk = rows // BLOCK
    view = lambda t: t.reshape(batch, seq, t.shape[-1])
    cur = lambda b, n: (b, n, 0)
    prev = lambda b, n: (b, jnp.maximum(n * prev_per_blk - 1, 0), 0)
    o = pl.pallas_call(
        functools.partial(_attn_b_kernel, rows=rows),
        grid=(batch, nblk),
        in_specs=[pl.BlockSpec(memory_space=pltpu.SMEM),
                  pl.BlockSpec((None, rows, WIDTH_B), cur),
                  pl.BlockSpec((None, BLOCK, 2 * KV_WIDTH_B), prev),
                  pl.BlockSpec((None, rows, 2 * KV_WIDTH_B), cur)],
        out_specs=pl.BlockSpec((None, rows, WIDTH_B), cur),
        out_shape=jax.ShapeDtypeStruct((batch, seq, WIDTH_B), BF16),
        compiler_params=_params("arbitrary", "arbitrary"),
        name="attn_b",
    )(sinks, view(qb), view(kvb), view(kvb))
    return o.reshape(batch * seq, WIDTH_B)


def _rms_gain(y, gain):
    return y * lax.rsqrt(jnp.mean(y * y, axis=-1, keepdims=True) + RMS_EPS) * gain


def _outproj_kernel(*refs, n_round):
    ya_ref, yb_ref, x_ref, w_ref, gna_ref, gnb_ref, g_ref, b_ref = refs[:8]
    out_ref = refs[8 + n_round]
    _round_slices(refs[8:8 + n_round], refs[9 + n_round:])
    ya = _rms_gain(ya_ref[...].astype(F32), gna_ref[...]).astype(BF16)
    yb = _rms_gain(yb_ref[...].astype(F32), gnb_ref[...]).astype(BF16)
    y = _dot(jnp.concatenate([ya, yb], axis=1), w_ref[...])
    out_ref[...] = _layernorm(ALPHA * x_ref[...] + y, g_ref[...], b_ref[...])


def _layer_vec(width, layer):
    return pl.BlockSpec((None, 1, width), lambda *_: (layer, 0, 0))


def _outproj(ya, yb, x, w, gna, gnb, g, b, layer, round_stacks=(), tm=512):
    m = x.shape[0]
    row = lambda i: (i, 0)
    round_in, round_out, round_shape = _round_job(round_stacks, layer + 1, m // tm)
    outs = pl.pallas_call(
        functools.partial(_outproj_kernel, n_round=len(round_stacks)),
        grid=(m // tm,),
        in_specs=[pl.BlockSpec((tm, WIDTH_A), row), pl.BlockSpec((tm, WIDTH_B), row),
                  pl.BlockSpec((tm, D_MODEL), row),
                  _resident_weight(w),
                  _layer_vec(WIDTH_A, layer), _layer_vec(WIDTH_B, layer),
                  _layer_vec(D_MODEL, layer), _layer_vec(D_MODEL, layer)] + round_in,
        out_specs=[pl.BlockSpec((tm, D_MODEL), row)] + round_out,
        out_shape=[jax.ShapeDtypeStruct((m, D_MODEL), F32)] + round_shape,
        compiler_params=_params("arbitrary"),
        name="outproj",
    )(ya, yb, x, w, gna, gnb, g, b, *round_stacks)
    return outs[0], outs[1:]


def _memkv_kernel(mem_ref, w_ref, kv_ref):
    kv_ref[...] = _dot(mem_ref[...].astype(BF16), w_ref[...].astype(BF16)).astype(BF16)


def _memkv(mem, w_mkv, tn=1024):
    rows = mem.shape[0]
    return pl.pallas_call(
        _memkv_kernel,
        grid=(DEPTH, 2 * D_MODEL // tn),
        in_specs=[pl.BlockSpec((rows, D_MODEL), lambda i, j: (0, 0)),
                  pl.BlockSpec((None, D_MODEL, tn), lambda i, j: (i, 0, j))],
        out_specs=pl.BlockSpec((None, rows, tn), lambda i, j: (i, 0, j)),
        out_shape=jax.ShapeDtypeStruct((DEPTH, rows, 2 * D_MODEL), BF16),
        compiler_params=_params("arbitrary", "arbitrary"),
        name="memkv",
    )(mem, w_mkv)


def _memattn_kernel(x_ref, wq_ref, k_ref, v_ref, wo_ref, g_ref, b_ref, out_ref):
    x = x_ref[...]
    q = (_dot(x.astype(BF16), wq_ref[...]) * MEM_HEAD_DIM ** -0.5).astype(BF16)
    heads = []
    for h in range(N_MEM_HEADS):
        cols = slice(h * MEM_HEAD_DIM, (h + 1) * MEM_HEAD_DIM)
        s = _dot_nt(q[:, cols], k_ref[:, cols])
        m = jnp.max(s, axis=-1, keepdims=True)
        p = jnp.exp(s - m)
        l = jnp.sum(p, axis=-1, keepdims=True)
        o = _dot(p.astype(BF16), v_ref[:, cols])
        heads.append((o * (1.0 / l)).astype(BF16))
    y = _dot(jnp.concatenate(heads, axis=1), wo_ref[...])
    out_ref[...] = _layernorm(ALPHA * x + y, g_ref[...], b_ref[...])


def _memattn(x, wq, kv, wo, g, b, batch, seq, layer, tm=512):
    n_mem = kv.shape[1] // batch
    per_batch = seq // tm
    row = lambda i: (i, 0)
    weight = _resident_weight(wq)
    kv_spec = lambda half: pl.BlockSpec((None, n_mem, D_MODEL),
                                        lambda i: (layer, i // per_batch, half))
    return pl.pallas_call(
        _memattn_kernel,
        grid=(batch * seq // tm,),
        in_specs=[pl.BlockSpec((tm, D_MODEL), row), weight, kv_spec(0), kv_spec(1), weight,
                  _layer_vec(D_MODEL, layer), _layer_vec(D_MODEL, layer)],
        out_specs=pl.BlockSpec((tm, D_MODEL), row),
        out_shape=jax.ShapeDtypeStruct((batch * seq, D_MODEL), F32),
        compiler_params=_params("arbitrary"),
        name="memattn",
    )(x, wq, kv, kv, wo, g, b)


def _mlp_kernel(x_ref, wu_ref, wd_ref, g_ref, b_ref, out_ref):
    j = pl.program_id(1)

    @pl.when(j == 0)
    def _():
        out_ref[...] = ALPHA * x_ref[...]

    h = jnp.maximum(_dot(x_ref[...].astype(BF16), wu_ref[...].astype(BF16)), 0.0)
    out_ref[...] += _dot((h * h).astype(BF16), wd_ref[...].astype(BF16))

    @pl.when(j == pl.num_programs(1) - 1)
    def _():
        out_ref[...] = _layernorm(out_ref[...], g_ref[...], b_ref[...])


def _mlp(x, wu, wd, g, b, layer, tm=1024, tf=512):
    m = x.shape[0]
    row = lambda i, j: (i, 0)
    return pl.pallas_call(
        _mlp_kernel,
        grid=(m // tm, D_FF // tf),
        in_specs=[pl.BlockSpec((tm, D_MODEL), row),
                  pl.BlockSpec((None, D_MODEL, tf), lambda i, j: (layer, 0, j)),
                  pl.BlockSpec((None, tf, D_MODEL), lambda i, j: (layer, j, 0)),
                  _layer_vec(D_MODEL, layer), _layer_vec(D_MODEL, layer)],
        out_specs=pl.BlockSpec((tm, D_MODEL), row),
        out_shape=jax.ShapeDtypeStruct((m, D_MODEL), F32),
        compiler_params=_params("arbitrary", "arbitrary"),
        name="mlp",
    )(x, wu, wd, g, b)


def _rope_lane_tables(positions, head_dim, rot_dim):
    half = rot_dim // 2
    inv_freq = ROPE_THETA ** (-jnp.arange(0, rot_dim, 2, dtype=F32) / rot_dim)
    ang = positions.astype(F32).reshape(-1, 1) * inv_freq
    cos, sin = jnp.cos(ang), jnp.sin(ang)
    rows = ang.shape[0]
    pad = jnp.zeros((rows, head_dim - rot_dim), F32)
    zero = jnp.zeros((rows, half), F32)
    reps = BLOCK // head_dim
    c = jnp.tile(jnp.concatenate([cos, cos, pad + 1.0], axis=1), (1, reps))
    s_up = jnp.tile(jnp.concatenate([zero, sin, pad], axis=1), (1, reps))
    s_dn = jnp.tile(jnp.concatenate([-sin, zero, pad], axis=1), (1, reps))
    return c, s_up, s_dn


def kernel(x, mem, positions, w_in, gn_a, gn_b, sinks, w_out, ln_mix_g, ln_mix_b, w_mq, w_mkv,
           w_mo, ln_mem_g, ln_mem_b, w_up, w_down, ln_ff_g, ln_ff_b):
    batch, seq, _ = x.shape
    tabs_a = _rope_lane_tables(positions, HEAD_DIM_A, ROT_DIM_A)
    tabs_b = _rope_lane_tables(positions, HEAD_DIM_B, ROT_DIM_B)
    w_in_i, w_out_i, w_mq_i, w_mo_i = _round_layer((w_in, w_out, w_mq, w_mo), 0)
    gn_a, gn_b, ln_mix_g, ln_mix_b, ln_mem_g, ln_mem_b, ln_ff_g, ln_ff_b = (
        v.reshape(DEPTH, 1, -1)
        for v in (gn_a, gn_b, ln_mix_g, ln_mix_b, ln_mem_g, ln_mem_b, ln_ff_g, ln_ff_b))
    kv = _memkv(mem.reshape(-1, D_MODEL), w_mkv)
    h = x.reshape(batch * seq, D_MODEL)
    for i in range(DEPTH):
        more = i + 1 < DEPTH
        (qa, ka, va, qb, kvb), next_in = _inproj(
            h, w_in_i, tabs_a, tabs_b, (w_in,) if more else (), i + 1)
        ya = _attn_a(qa, ka, va, batch, seq)
        yb = _attn_b(sinks[i], qb, kvb, batch, seq)
        h, next_rest = _outproj(ya, yb, h, w_out_i, gn_a, gn_b, ln_mix_g, ln_mix_b, i,
                                (w_out, w_mq, w_mo) if more else ())
        h = _memattn(h, w_mq_i, kv, w_mo_i, ln_mem_g, ln_mem_b, batch, seq, i)
        h = _mlp(h, w_up, w_down, ln_ff_g, ln_ff_b, i)
        if more:
            (w_in_i,), (w_out_i, w_mq_i, w_mo_i) = next_in, next_rest
    return h.reshape(batch, seq, D_MODEL)
```

```python
import functools
import math

import jax
import jax.numpy as jnp
from jax import lax
from jax.experimental import pallas as pl
from jax.experimental.pallas import tpu as pltpu

D_MODEL = 2048
DEPTH = 4
N_MEM_HEADS = 4
MEM_HEAD_DIM = D_MODEL // N_MEM_HEADS
HEAD_DIM_A = 128
N_HEADS_A = 8
WIDTH_A = N_HEADS_A * HEAD_DIM_A
HEAD_DIM_B = 64
N_Q_HEADS_B = 16
N_KV_HEADS_B = 2
WIDTH_B = N_Q_HEADS_B * HEAD_DIM_B
KV_WIDTH_B = N_KV_HEADS_B * HEAD_DIM_B
ROPE_THETA = 500000.0
ROT_DIM_A = HEAD_DIM_A // 4
ROT_DIM_B = HEAD_DIM_B // 4
D_FF = 4 * D_MODEL
BLOCK = 128
ALPHA = (2 * DEPTH) ** 0.25
LN_EPS = 1e-5
RMS_EPS = 1e-6
NEG_INF = -1e30
LOG2E = math.log2(math.e)

PIECE = 16
GROUP = PIECE * PIECE
GROUPS_PER_BODY = 5
LN_SUBTILE = 256

BF16 = jnp.bfloat16
F32 = jnp.float32

VMEM_LIMIT_BYTES = 58 * 1024 * 1024


def _params(*semantics):
    return pltpu.CompilerParams(dimension_semantics=semantics,
                                vmem_limit_bytes=VMEM_LIMIT_BYTES)


def _single_buffered(block_shape, index_map):
    return pl.BlockSpec(block_shape, index_map, pipeline_mode=pl.Buffered(1))


def _resident_weight(w):
    return _single_buffered((None,) + w.shape[1:], lambda *_: (0, 0, 0))


def _round_job(stacks, layer, steps):
    rows = [w.shape[1] // steps for w in stacks]
    assert all(r * steps == w.shape[1] and r % PIECE == 0 for r, w in zip(rows, stacks))
    blocks = [(None, r, w.shape[2]) for r, w in zip(rows, stacks)]
    return ([pl.BlockSpec(blk, lambda i: (layer, i, 0)) for blk in blocks],
            [pl.BlockSpec(blk, lambda i: (0, i, 0)) for blk in blocks],
            [jax.ShapeDtypeStruct((1,) + w.shape[1:], BF16) for w in stacks])


def _round_slices(src_refs, dst_refs):
    for src, dst in zip(src_refs, dst_refs):
        dst[...] = src[...].astype(BF16)


def _round_kernel(*refs):
    _round_slices(refs[:len(refs) // 2], refs[len(refs) // 2:])


def _round_layer(stacks, layer, steps=8):
    in_specs, out_specs, out_shape = _round_job(stacks, layer, steps)
    return pl.pallas_call(
        _round_kernel, grid=(steps,), in_specs=in_specs, out_specs=out_specs,
        out_shape=out_shape, compiler_params=_params("arbitrary"), name="round_weights",
    )(*stacks)


def _layernorm(z, g, b):
    mu = jnp.mean(z, axis=-1, keepdims=True)
    zc = z - mu
    var = jnp.mean(zc * zc, axis=-1, keepdims=True)
    return zc * lax.rsqrt(var + LN_EPS) * g + b


def _dot_nt(a, b):
    return lax.dot_general(a, b, (((1,), (1,)), ((), ())), preferred_element_type=F32)


def _dot(a, b):
    return jnp.dot(a, b, preferred_element_type=F32)


def _rope(t, c, s_up, s_dn, half):
    return (t * c + pltpu.roll(t, half, 1) * s_up
            + pltpu.roll(t, BLOCK - half, 1) * s_dn)


def _inproj_kernel(*refs, n_round):
    x_ref, w_ref, ca_ref, ua_ref, da_ref, cb_ref, ub_ref, db_ref = refs[:8]
    qa_ref, ka_ref, va_ref, qb_ref, kvb_ref = refs[8 + n_round:13 + n_round]
    _round_slices(refs[8:8 + n_round], refs[13 + n_round:])
    x = x_ref[...].astype(BF16)
    ca, ua, da = ca_ref[...], ua_ref[...], da_ref[...]
    cb, ub, db = cb_ref[...], ub_ref[...], db_ref[...]
    scale_a = HEAD_DIM_A ** -0.5 * LOG2E
    scale_b = HEAD_DIM_B ** -0.5 * LOG2E
    chunk = 4 * BLOCK

    def proj(col, width):
        return _dot(x, w_ref[:, col:col + width])

    for c0 in range(0, WIDTH_A, chunk):
        hq = proj(c0, chunk)
        hk = proj(WIDTH_A + c0, chunk)
        for j in range(0, chunk, BLOCK):
            sl = slice(c0 + j, c0 + j + BLOCK)
            qa_ref[:, sl] = (_rope(hq[:, j:j + BLOCK], ca, ua, da, ROT_DIM_A // 2)
                             * scale_a).astype(BF16)
            ka_ref[:, sl] = _rope(hk[:, j:j + BLOCK], ca, ua, da,
                                  ROT_DIM_A // 2).astype(BF16)
        va_ref[:, c0:c0 + chunk] = proj(2 * WIDTH_A + c0, chunk).astype(BF16)
    for c0 in range(0, WIDTH_B, chunk):
        hq = proj(3 * WIDTH_A + c0, chunk)
        for j in range(0, chunk, BLOCK):
            qb_ref[:, c0 + j:c0 + j + BLOCK] = (
                _rope(hq[:, j:j + BLOCK], cb, ub, db, ROT_DIM_B // 2) * scale_b).astype(BF16)
    hkv = proj(3 * WIDTH_A + WIDTH_B, 2 * KV_WIDTH_B)
    kvb_ref[:, :KV_WIDTH_B] = _rope(hkv[:, :KV_WIDTH_B], cb, ub, db,
                                    ROT_DIM_B // 2).astype(BF16)
    kvb_ref[:, KV_WIDTH_B:] = hkv[:, KV_WIDTH_B:].astype(BF16)


def _inproj(x, w, tabs_a, tabs_b, round_stacks=(), round_layer=0, tm=512):
    m = x.shape[0]
    row = lambda i: (i, 0)
    tab_spec = pl.BlockSpec((tm, BLOCK), row)
    out = lambda width: jax.ShapeDtypeStruct((m, width), BF16)
    round_in, round_out, round_shape = _round_job(round_stacks, round_layer, m // tm)
    outs = pl.pallas_call(
        functools.partial(_inproj_kernel, n_round=len(round_stacks)),
        grid=(m // tm,),
        in_specs=[pl.BlockSpec((tm, D_MODEL), row), _resident_weight(w)] + [tab_spec] * 6
        + round_in,
        out_specs=[pl.BlockSpec((tm, WIDTH_A), row)] * 3
        + [pl.BlockSpec((tm, WIDTH_B), row), pl.BlockSpec((tm, 2 * KV_WIDTH_B), row)]
        + round_out,
        out_shape=[out(WIDTH_A)] * 3 + [out(WIDTH_B), out(2 * KV_WIDTH_B)] + round_shape,
        compiler_params=_params("arbitrary"),
        name="inproj",
    )(x, w, *tabs_a, *tabs_b, *round_stacks)
    return outs[:5], outs[5:]


def _band_mask(max_dist):
    qi = lax.broadcasted_iota(jnp.int32, (BLOCK, 2 * BLOCK), 0) + BLOCK
    kj = lax.broadcasted_iota(jnp.int32, (BLOCK, 2 * BLOCK), 1)
    dist = qi - kj
    return (dist >= 0) & (dist <= max_dist)


def _softmax_block(q, k, v, bias):
    s = _dot_nt(q, k) + bias
    m = jnp.max(s, axis=-1, keepdims=True)
    p = jnp.exp2(s - m)
    l = jnp.sum(p, axis=-1, keepdims=True)
    return m, l, _dot(p.astype(BF16), v)


def _softmax_block_onto(q, k, v, bias, m_prev, l_prev, acc_prev):
    s = _dot_nt(q, k) + bias
    m = jnp.maximum(m_prev, jnp.max(s, axis=-1, keepdims=True))
    scale_prev = jnp.exp2(m_prev - m)
    p = jnp.exp2(s - jnp.concatenate([m] * (s.shape[1] // BLOCK), axis=1))
    l = scale_prev * l_prev + jnp.sum(p, axis=-1, keepdims=True)
    return m, l, scale_prev * acc_prev + _dot(p.astype(BF16), v)


def _piece(offset):
    return pl.ds(offset if isinstance(offset, int) else pl.multiple_of(offset, PIECE), PIECE)


def _gather(ref, offsets, cols):
    return jnp.concatenate([ref[_piece(o), cols] for o in offsets], axis=0)


def _scatter(ref, offsets, cols, val):
    for i, o in enumerate(offsets):
        ref[_piece(o), cols] = val[i * PIECE:(i + 1) * PIECE]


def _attn_a_kernel(perm_ref, q_ref, k_ref, v_ref, o_ref,
                   qt_ref, kt_ref, vt_ref, acc4_ref, m4_ref, l4_ref, acc16_ref, m16_ref, l16_ref,
                   bias_ref, *, heads):
    seq = q_ref.shape[0]
    n_groups = seq // GROUP
    perm = perm_ref[...]
    head_cols = [slice(h * HEAD_DIM_A, (h + 1) * HEAD_DIM_A) for h in range(heads)]
    all_lanes = slice(0, BLOCK)
    tile = lambda col: jnp.broadcast_to(col, (BLOCK, BLOCK))

    def i4_of(idx):
        return 64 * (idx >> 6) + 4 * (idx & 15) + ((idx >> 4) & 3)

    qi = lax.broadcasted_iota(jnp.int32, (BLOCK, 2 * BLOCK), 0)
    kj = lax.broadcasted_iota(jnp.int32, (BLOCK, 2 * BLOCK), 1)
    dist4 = BLOCK + i4_of(qi) - i4_of(kj)
    for i, band in enumerate((_band_mask(BLOCK), (dist4 >= 0) & (dist4 <= BLOCK))):
        bias_ref[i] = jnp.where(band, 0.0, NEG_INF)
        bias_ref[2 + i] = jnp.where(band & (kj >= BLOCK), 0.0, NEG_INF)

    for g in range(n_groups):
        rows = slice(g * GROUP, (g + 1) * GROUP)
        for src, dst in ((q_ref, qt_ref), (k_ref, kt_ref), (v_ref, vt_ref)):
            dst[rows, :] = _dot(perm, src[rows, :]).astype(BF16)

    def transposed_block(q_off, prev_off, band, acc_ref, m_ref, l_ref):
        if prev_off is None:
            k_off, bias = q_off + q_off, bias_ref[2 + band]
        else:
            k_off, bias = prev_off + q_off, bias_ref[band]
        for h, cols in enumerate(head_cols):
            m, l, acc = _softmax_block(_gather(qt_ref, q_off, cols), _gather(kt_ref, k_off, cols),
                                       _gather(vt_ref, k_off, cols), bias)
            _scatter(acc_ref, q_off, cols, acc)
            _scatter(m_ref.at[h], q_off, all_lanes, tile(m))
            _scatter(l_ref.at[h], q_off, all_lanes, tile(l))

    for r4 in range(4):
        for n in range(seq // 4 // BLOCK):
            q_off = [GROUP * (2 * n + g) + PIECE * (r4 + 4 * k)
                     for g in range(2) for k in range(4)]
            prev_off = [o - 2 * GROUP for o in q_off] if n else None
            transposed_block(q_off, prev_off, 1, acc4_ref, m4_ref, l4_ref)

    groups_per_block = BLOCK // PIECE
    for r in range(PIECE):
        for n in range(seq // PIECE // BLOCK):
            q_off = [GROUP * (groups_per_block * n + g) + PIECE * r
                     for g in range(groups_per_block)]
            prev_off = [o - GROUP * groups_per_block for o in q_off] if n else None
            transposed_block(q_off, prev_off, 0, acc16_ref, m16_ref, l16_ref)

    def dilation1(g, first):
        g = jnp.int32(g)
        rows = pl.ds(pl.multiple_of(g * GROUP, GROUP), GROUP)
        accs, stats = [], []
        for h, cols in enumerate(head_cols):
            m4, m16 = m4_ref[h, rows, :], m16_ref[h, rows, :]
            m = jnp.maximum(m4, m16).astype(BF16)
            w4, w16 = jnp.exp2(m4 - m.astype(F32)), jnp.exp2(m16 - m.astype(F32))
            l = w4 * l4_ref[h, rows, :] + w16 * l16_ref[h, rows, :]
            l_hi = l.astype(BF16)
            accs.append((w4 * acc4_ref[rows, cols] + w16 * acc16_ref[rows, cols]).astype(BF16))
            stats += [m, l_hi, (l - l_hi.astype(F32)).astype(BF16)]
        carry = _dot(perm, jnp.concatenate(accs + stats, axis=1))
        for j in range(GROUP // BLOCK):
            local = slice(j * BLOCK, (j + 1) * BLOCK)
            r0 = pl.multiple_of(g * GROUP + j * BLOCK, BLOCK)
            for h, cols in enumerate(head_cols):
                stats = heads * HEAD_DIM_A + 3 * BLOCK * h
                m_prev = carry[local, stats:stats + BLOCK]
                l_prev = (carry[local, stats + BLOCK:stats + 2 * BLOCK]
                          + carry[local, stats + 2 * BLOCK:stats + 3 * BLOCK])
                if first and j == 0:
                    p0, bias = (r0, r0), bias_ref[2]
                else:
                    p0 = pl.multiple_of(g * GROUP + (j - 1) * BLOCK, BLOCK)
                    p0, bias = (p0, r0), bias_ref[0]
                k2 = jnp.concatenate([k_ref[pl.ds(p, BLOCK), cols] for p in p0], axis=0)
                v2 = jnp.concatenate([v_ref[pl.ds(p, BLOCK), cols] for p in p0], axis=0)
                _, l, acc = _softmax_block_onto(q_ref[pl.ds(r0, BLOCK), cols], k2, v2, bias,
                                                m_prev, l_prev, carry[local, cols])
                o_ref[pl.ds(r0, BLOCK), cols] = (acc * (1.0 / l)).astype(BF16)

    def later_groups(t, carry):
        for i in range(GROUPS_PER_BODY):
            dilation1(1 + GROUPS_PER_BODY * t + i, False)
        return carry

    dilation1(0, True)
    lax.fori_loop(0, (n_groups - 1) // GROUPS_PER_BODY, later_groups, 0)


def _group_transpose_matrix():
    idx = jnp.arange(GROUP)
    return (idx[:, None] == PIECE * (idx[None, :] % PIECE) + idx[None, :] // PIECE).astype(BF16)


def _attn_a(qa, ka, va, batch, seq, heads=2):
    assert (seq // GROUP - 1) % GROUPS_PER_BODY == 0
    width = heads * HEAD_DIM_A
    view = lambda t: t.reshape(batch, seq, WIDTH_A)
    blk = pl.BlockSpec((None, seq, width), lambda b, h: (b, 0, h))
    stat = pltpu.VMEM((heads, seq, BLOCK), F32)
    o = pl.pallas_call(
        functools.partial(_attn_a_kernel, heads=heads),
        grid=(batch, N_HEADS_A // heads),
        in_specs=[pl.BlockSpec((GROUP, GROUP), lambda b, h: (0, 0)), blk, blk, blk],
        out_specs=blk,
        out_shape=jax.ShapeDtypeStruct((batch, seq, WIDTH_A), BF16),
        scratch_shapes=[pltpu.VMEM((seq, width), BF16)] * 3
        + [pltpu.VMEM((seq, width), F32), stat, stat] * 2
        + [pltpu.VMEM((4, BLOCK, 2 * BLOCK), F32)],
        compiler_params=_params("arbitrary", "arbitrary"),
        name="attn_a",
    )(_group_transpose_matrix(), view(qa), view(ka), view(va))
    return o.reshape(batch * seq, WIDTH_A)


def _attn_b_kernel(sink_ref, q_ref, kvp_ref, kv_ref, o_ref, *, rows):
    stack = lambda t: jnp.concatenate([t, t], axis=0)
    qi = lax.broadcasted_iota(jnp.int32, (BLOCK, BLOCK), 0)
    kj = lax.broadcasted_iota(jnp.int32, (BLOCK, BLOCK), 1)
    from_prev = stack(kj > qi)
    no_prev = jnp.where(pl.program_id(1) == 0, NEG_INF, 0.0)
    low = kj < HEAD_DIM_B
    low2 = stack(low)
    row2 = lax.broadcasted_iota(jnp.int32, (2 * BLOCK, 1), 0)
    heads_per_group = N_Q_HEADS_B // N_KV_HEADS_B
    for j in range(rows // BLOCK):
        r0 = j * BLOCK
        if j == 0:
            kv2 = jnp.concatenate([kvp_ref[...], kv_ref[:BLOCK, :]], axis=0)
        else:
            kv2 = kv_ref[r0 - BLOCK:r0 + BLOCK, :]
        kv2 = kv2.astype(F32)
        k2, v2 = kv2[:, :KV_WIDTH_B], kv2[:, KV_WIDTH_B:]
        k2r, v2r = pltpu.roll(k2, HEAD_DIM_B, 1), pltpu.roll(v2, HEAD_DIM_B, 1)
        kdup = [jnp.where(low2, k2, k2r).astype(BF16), jnp.where(low2, k2r, k2).astype(BF16)]
        vdup = [jnp.where(low2, v2, v2r).astype(BF16), jnp.where(low2, v2r, v2).astype(BF16)]
        for t in range(WIDTH_B // BLOCK):
            g = (2 * t) // heads_per_group
            cols = slice(t * BLOCK, (t + 1) * BLOCK)
            q = q_ref[r0:r0 + BLOCK, cols]
            zero = jnp.zeros_like(q)
            q2 = jnp.concatenate([jnp.where(low, q, zero), jnp.where(low, zero, q)], axis=0)
            s2 = _dot_nt(q2, kdup[g])
            s_prev = s2[:, :BLOCK] + no_prev if j == 0 else s2[:, :BLOCK]
            s = jnp.where(from_prev, s_prev, s2[:, BLOCK:])
            m = jnp.max(s, axis=-1, keepdims=True)
            p = jnp.exp2(s - m)
            l = jnp.sum(p, axis=-1, keepdims=True)
            p2 = jnp.concatenate([jnp.where(from_prev, p, 0.0), jnp.where(from_prev, 0.0, p)],
                                 axis=1)
            acc = _dot(p2.astype(BF16), vdup[g])
            sink = jnp.where(row2 < BLOCK, sink_ref[2 * t], sink_ref[2 * t + 1]) * LOG2E
            m2 = jnp.maximum(m, sink)
            c = jnp.exp2(m - m2)
            den = l * c + jnp.exp2(sink - m2)
            y = acc * (c / den)
            o_ref[r0:r0 + BLOCK, cols] = jnp.where(low, y[:BLOCK], y[BLOCK:]).astype(BF16)


def _attn_b(sinks, qb, kvb, batch, seq, rows=1024):
    nblk = seq // rows
    prev_per_blk = rows // BLOCK
    view = lambda t: t.reshape(batch, seq, t.shape[-1])
    cur = lambda b, n: (b, n, 0)
    prev = lambda b, n: (b, jnp.maximum(n * prev_per_blk - 1, 0), 0)
    o = pl.pallas_call(
        functools.partial(_attn_b_kernel, rows=rows),
        grid=(batch, nblk),
        in_specs=[pl.BlockSpec(memory_space=pltpu.SMEM),
                  pl.BlockSpec((None, rows, WIDTH_B), cur),
                  pl.BlockSpec((None, BLOCK, 2 * KV_WIDTH_B), prev),
                  pl.BlockSpec((None, rows, 2 * KV_WIDTH_B), cur)],
        out_specs=pl.BlockSpec((None, rows, WIDTH_B), cur),
        out_shape=jax.ShapeDtypeStruct((batch, seq, WIDTH_B), BF16),
        compiler_params=_params("arbitrary", "arbitrary"),
        name="attn_b",
    )(sinks, view(qb), view(kvb), view(kvb))
    return o.reshape(batch * seq, WIDTH_B)


def _rms_gain(y, gain):
    return y * lax.rsqrt(jnp.mean(y * y, axis=-1, keepdims=True) + RMS_EPS) * gain


def _outproj_kernel(*refs, n_round):
    ya_ref, yb_ref, x_ref, w_ref, gna_ref, gnb_ref, g_ref, b_ref = refs[:8]
    out_ref = refs[8 + n_round]
    _round_slices(refs[8:8 + n_round], refs[9 + n_round:])
    ya = _rms_gain(ya_ref[...].astype(F32), gna_ref[...]).astype(BF16)
    yb = _rms_gain(yb_ref[...].astype(F32), gnb_ref[...]).astype(BF16)
    y = _dot(jnp.concatenate([ya, yb], axis=1), w_ref[...])
    out_ref[...] = _layernorm(ALPHA * x_ref[...] + y, g_ref[...], b_ref[...])


def _layer_vec(width, layer):
    return pl.BlockSpec((None, 1, width), lambda *_: (layer, 0, 0))


def _outproj(ya, yb, x, w, gna, gnb, g, b, layer, round_stacks=(), tm=512):
    m = x.shape[0]
    row = lambda i: (i, 0)
    round_in, round_out, round_shape = _round_job(round_stacks, layer + 1, m // tm)
    outs = pl.pallas_call(
        functools.partial(_outproj_kernel, n_round=len(round_stacks)),
        grid=(m // tm,),
        in_specs=[pl.BlockSpec((tm, WIDTH_A), row), pl.BlockSpec((tm, WIDTH_B), row),
                  pl.BlockSpec((tm, D_MODEL), row),
                  _resident_weight(w),
                  _layer_vec(WIDTH_A, layer), _layer_vec(WIDTH_B, layer),
                  _layer_vec(D_MODEL, layer), _layer_vec(D_MODEL, layer)] + round_in,
        out_specs=[pl.BlockSpec((tm, D_MODEL), row)] + round_out,
        out_shape=[jax.ShapeDtypeStruct((m, D_MODEL), F32)] + round_shape,
        compiler_params=_params("arbitrary"),
        name="outproj",
    )(ya, yb, x, w, gna, gnb, g, b, *round_stacks)
    return outs[0], outs[1:]


def _memkv_kernel(mem_ref, w_ref, kv_ref):
    kv_ref[...] = _dot(mem_ref[...].astype(BF16), w_ref[...].astype(BF16)).astype(BF16)


def _memkv(mem, w_mkv, tn=1024):
    rows = mem.shape[0]
    return pl.pallas_call(
        _memkv_kernel,
        grid=(DEPTH, 2 * D_MODEL // tn),
        in_specs=[pl.BlockSpec((rows, D_MODEL), lambda i, j: (0, 0)),
                  pl.BlockSpec((None, D_MODEL, tn), lambda i, j: (i, 0, j))],
        out_specs=pl.BlockSpec((None, rows, tn), lambda i, j: (i, 0, j)),
        out_shape=jax.ShapeDtypeStruct((DEPTH, rows, 2 * D_MODEL), BF16),
        compiler_params=_params("arbitrary", "arbitrary"),
        name="memkv",
    )(mem, w_mkv)


def _memattn_kernel(x_ref, wq_ref, k_ref, v_ref, wo_ref, g_ref, b_ref, out_ref):
    x = x_ref[...]
    q = (_dot(x.astype(BF16), wq_ref[...]) * MEM_HEAD_DIM ** -0.5).astype(BF16)
    heads = []
    for h in range(N_MEM_HEADS):
        cols = slice(h * MEM_HEAD_DIM, (h + 1) * MEM_HEAD_DIM)
        s = _dot_nt(q[:, cols], k_ref[:, cols])
        m = jnp.max(s, axis=-1, keepdims=True)
        p = jnp.exp(s - m)
        l = jnp.sum(p, axis=-1, keepdims=True)
        o = _dot(p.astype(BF16), v_ref[:, cols])
        heads.append((o * (1.0 / l)).astype(BF16))
    y = _dot(jnp.concatenate(heads, axis=1), wo_ref[...])
    out_ref[...] = _layernorm(ALPHA * x + y, g_ref[...], b_ref[...])


def _memattn(x, wq, kv, wo, g, b, batch, seq, layer, tm=512):
    n_mem = kv.shape[1] // batch
    per_batch = seq // tm
    row = lambda i: (i, 0)
    weight = _resident_weight(wq)
    kv_spec = lambda half: pl.BlockSpec((None, n_mem, D_MODEL),
                                        lambda i: (layer, i // per_batch, half))
    return pl.pallas_call(
        _memattn_kernel,
        grid=(batch * seq // tm,),
        in_specs=[pl.BlockSpec((tm, D_MODEL), row), weight, kv_spec(0), kv_spec(1), weight,
                  _layer_vec(D_MODEL, layer), _layer_vec(D_MODEL, layer)],
        out_specs=pl.BlockSpec((tm, D_MODEL), row),
        out_shape=jax.ShapeDtypeStruct((batch * seq, D_MODEL), F32),
        compiler_params=_params("arbitrary"),
        name="memattn",
    )(x, wq, kv, kv, wo, g, b)


def _mlp_kernel(x_ref, wu_ref, wd_ref, g_ref, b_ref, out_ref):
    j = pl.program_id(1)
    last = pl.num_programs(1) - 1

    def chunk(row_tiles):
        wu, wd = wu_ref[...].astype(BF16), wd_ref[...].astype(BF16)
        for rows in row_tiles:
            h = jnp.maximum(_dot(x_ref[rows, :].astype(BF16), wu), 0.0)
            yield rows, _dot((h * h).astype(BF16), wd)

    whole = [slice(None)]

    @pl.when(j == 0)
    def _():
        for rows, part in chunk(whole):
            out_ref[rows, :] = ALPHA * x_ref[rows, :] + part

    @pl.when((j > 0) & (j < last))
    def _():
        for rows, part in chunk(whole):
            out_ref[rows, :] += part

    @pl.when(j == last)
    def _():
        tiles = [slice(r, r + LN_SUBTILE) for r in range(0, x_ref.shape[0], LN_SUBTILE)]
        for rows, part in chunk(tiles):
            out_ref[rows, :] = _layernorm(out_ref[rows, :] + part, g_ref[...], b_ref[...])


def _mlp(x, wu, wd, g, b, layer, tm=1024, tf=512):
    m = x.shape[0]
    row = lambda i, j: (i, 0)
    return pl.pallas_call(
        _mlp_kernel,
        grid=(m // tm, D_FF // tf),
        in_specs=[pl.BlockSpec((tm, D_MODEL), row),
                  pl.BlockSpec((None, D_MODEL, tf), lambda i, j: (layer, 0, j)),
                  pl.BlockSpec((None, tf, D_MODEL), lambda i, j: (layer, j, 0)),
                  _layer_vec(D_MODEL, layer), _layer_vec(D_MODEL, layer)],
        out_specs=pl.BlockSpec((tm, D_MODEL), row),
        out_shape=jax.ShapeDtypeStruct((m, D_MODEL), F32),
        compiler_params=_params("arbitrary", "arbitrary"),
        name="mlp",
    )(x, wu, wd, g, b)


def _rope_lane_tables(positions, head_dim, rot_dim):
    half = rot_dim // 2
    inv_freq = ROPE_THETA ** (-jnp.arange(0, rot_dim, 2, dtype=F32) / rot_dim)
    ang = positions.astype(F32).reshape(-1, 1) * inv_freq
    cos, sin = jnp.cos(ang), jnp.sin(ang)
    rows = ang.shape[0]
    pad = jnp.zeros((rows, head_dim - rot_dim), F32)
    zero = jnp.zeros((rows, half), F32)
    reps = BLOCK // head_dim
    c = jnp.tile(jnp.concatenate([cos, cos, pad + 1.0], axis=1), (1, reps))
    s_up = jnp.tile(jnp.concatenate([zero, sin, pad], axis=1), (1, reps))
    s_dn = jnp.tile(jnp.concatenate([-sin, zero, pad], axis=1), (1, reps))
    return c, s_up, s_dn


def kernel(x, mem, positions, w_in, gn_a, gn_b, sinks, w_out, ln_mix_g, ln_mix_b, w_mq, w_mkv,
           w_mo, ln_mem_g, ln_mem_b, w_up, w_down, ln_ff_g, ln_ff_b):
    batch, seq, _ = x.shape
    tabs_a = _rope_lane_tables(positions, HEAD_DIM_A, ROT_DIM_A)
    tabs_b = _rope_lane_tables(positions, HEAD_DIM_B, ROT_DIM_B)
    w_in_i, w_out_i, w_mq_i, w_mo_i = _round_layer((w_in, w_out, w_mq, w_mo), 0)
    gn_a, gn_b, ln_mix_g, ln_mix_b, ln_mem_g, ln_mem_b, ln_ff_g, ln_ff_b = (
        v.reshape(DEPTH, 1, -1)
        for v in (gn_a, gn_b, ln_mix_g, ln_mix_b, ln_mem_g, ln_mem_b, ln_ff_g, ln_ff_b))
    kv = _memkv(mem.reshape(-1, D_MODEL), w_mkv)
    h = x.reshape(batch * seq, D_MODEL)
    for i in range(DEPTH):
        more = i + 1 < DEPTH
        (qa, ka, va, qb, kvb), next_in = _inproj(
            h, w_in_i, tabs_a, tabs_b, (w_in,) if more else (), i + 1)
        ya = _attn_a(qa, ka, va, batch, seq)
        yb = _attn_b(sinks[i], qb, kvb, batch, seq)
        h, next_rest = _outproj(ya, yb, h, w_out_i, gn_a, gn_b, ln_mix_g, ln_mix_b, i,
                                (w_out, w_mq, w_mo) if more else ())
        h = _memattn(h, w_mq_i, kv, w_mo_i, ln_mem_g, ln_mem_b, batch, seq, i)
        h = _mlp(h, w_up, w_down, ln_ff_g, ln_ff_b, i)
        if more:
            (w_in_i,), (w_out_i, w_mq_i, w_mo_i) = next_in, next_rest
    return h.reshape(batch, seq, D_MODEL)
```

```python
import functools
import math

import jax
import jax.numpy as jnp
from jax import lax
from jax.experimental import pallas as pl
from jax.experimental.pallas import tpu as pltpu

D_MODEL = 2048
DEPTH = 4
N_MEM_HEADS = 4
MEM_HEAD_DIM = D_MODEL // N_MEM_HEADS
HEAD_DIM_A = 128
N_HEADS_A = 8
WIDTH_A = N_HEADS_A * HEAD_DIM_A
HEAD_DIM_B = 64
N_Q_HEADS_B = 16
N_KV_HEADS_B = 2
WIDTH_B = N_Q_HEADS_B * HEAD_DIM_B
KV_WIDTH_B = N_KV_HEADS_B * HEAD_DIM_B
ROPE_THETA = 500000.0
ROT_DIM_A = HEAD_DIM_A // 4
ROT_DIM_B = HEAD_DIM_B // 4
D_FF = 4 * D_MODEL
BLOCK = 128
ALPHA = (2 * DEPTH) ** 0.25
LN_EPS = 1e-5
RMS_EPS = 1e-6
NEG_INF = -1e30
LOG2E = math.log2(math.e)

PIECE = 16
GROUP = PIECE * PIECE
GROUPS_PER_BODY = 5

BF16 = jnp.bfloat16
F32 = jnp.float32

VMEM_LIMIT_BYTES = 58 * 1024 * 1024


def _params(*semantics):
    return pltpu.CompilerParams(dimension_semantics=semantics,
                                vmem_limit_bytes=VMEM_LIMIT_BYTES)


def _single_buffered(block_shape, index_map):
    return pl.BlockSpec(block_shape, index_map, pipeline_mode=pl.Buffered(1))


def _resident_weight(w):
    return _single_buffered((None,) + w.shape[1:], lambda *_: (0, 0, 0))


def _round_job(stacks, layers, steps):
    rows = [w.shape[1] // steps for w in stacks]
    assert all(r * steps == w.shape[1] and r % PIECE == 0 for r, w in zip(rows, stacks))
    blocks = [(None, r, w.shape[2]) for r, w in zip(rows, stacks)]
    return ([pl.BlockSpec(blk, lambda i, layer=layer: (layer, i, 0))
             for blk, layer in zip(blocks, layers)],
            [pl.BlockSpec(blk, lambda i: (0, i, 0)) for blk in blocks],
            [jax.ShapeDtypeStruct((1,) + w.shape[1:], BF16) for w in stacks])


def _round_slices(src_refs, dst_refs):
    for src, dst in zip(src_refs, dst_refs):
        dst[...] = src[...].astype(BF16)


def _round_kernel(*refs):
    _round_slices(refs[:len(refs) // 2], refs[len(refs) // 2:])


def _round_layer(stacks, layer, steps=8):
    in_specs, out_specs, out_shape = _round_job(stacks, [layer] * len(stacks), steps)
    return pl.pallas_call(
        _round_kernel, grid=(steps,), in_specs=in_specs, out_specs=out_specs,
        out_shape=out_shape, compiler_params=_params("arbitrary"), name="round_weights",
    )(*stacks)


def _layernorm(z, g, b):
    mu = jnp.mean(z, axis=-1, keepdims=True)
    zc = z - mu
    var = jnp.mean(zc * zc, axis=-1, keepdims=True)
    return zc * lax.rsqrt(var + LN_EPS) * g + b


def _dot_nt(a, b):
    return lax.dot_general(a, b, (((1,), (1,)), ((), ())), preferred_element_type=F32)


def _dot(a, b):
    return jnp.dot(a, b, preferred_element_type=F32)


def _rope(t, c, s_up, s_dn, half):
    return (t * c + pltpu.roll(t, half, 1) * s_up
            + pltpu.roll(t, BLOCK - half, 1) * s_dn)


def _inproj_kernel(*refs, n_round):
    x_ref, w_ref, ca_ref, ua_ref, da_ref, cb_ref, ub_ref, db_ref = refs[:8]
    qa_ref, ka_ref, va_ref, qb_ref, kvb_ref = refs[8 + n_round:13 + n_round]
    _round_slices(refs[8:8 + n_round], refs[13 + n_round:])
    x = x_ref[...].astype(BF16)
    ca, ua, da = ca_ref[...], ua_ref[...], da_ref[...]
    cb, ub, db = cb_ref[...], ub_ref[...], db_ref[...]
    scale_a = HEAD_DIM_A ** -0.5 * LOG2E
    scale_b = HEAD_DIM_B ** -0.5 * LOG2E
    chunk = 4 * BLOCK

    def proj(col, width):
        return _dot(x, w_ref[:, col:col + width])

    for c0 in range(0, WIDTH_A, chunk):
        hq = proj(c0, chunk)
        hk = proj(WIDTH_A + c0, chunk)
        for j in range(0, chunk, BLOCK):
            sl = slice(c0 + j, c0 + j + BLOCK)
            qa_ref[:, sl] = (_rope(hq[:, j:j + BLOCK], ca, ua, da, ROT_DIM_A // 2)
                             * scale_a).astype(BF16)
            ka_ref[:, sl] = _rope(hk[:, j:j + BLOCK], ca, ua, da,
                                  ROT_DIM_A // 2).astype(BF16)
        va_ref[:, c0:c0 + chunk] = proj(2 * WIDTH_A + c0, chunk).astype(BF16)
    for c0 in range(0, WIDTH_B, chunk):
        hq = proj(3 * WIDTH_A + c0, chunk)
        for j in range(0, chunk, BLOCK):
            qb_ref[:, c0 + j:c0 + j + BLOCK] = (
                _rope(hq[:, j:j + BLOCK], cb, ub, db, ROT_DIM_B // 2) * scale_b).astype(BF16)
    hkv = proj(3 * WIDTH_A + WIDTH_B, 2 * KV_WIDTH_B)
    kvb_ref[:, :KV_WIDTH_B] = _rope(hkv[:, :KV_WIDTH_B], cb, ub, db,
                                    ROT_DIM_B // 2).astype(BF16)
    kvb_ref[:, KV_WIDTH_B:] = hkv[:, KV_WIDTH_B:].astype(BF16)


def _inproj(x, w, tabs_a, tabs_b, round_stacks=(), round_layers=(), tm=512):
    m = x.shape[0]
    row = lambda i: (i, 0)
    tab_spec = pl.BlockSpec((tm, BLOCK), row)
    out = lambda width: jax.ShapeDtypeStruct((m, width), BF16)
    round_in, round_out, round_shape = _round_job(round_stacks, round_layers, m // tm)
    outs = pl.pallas_call(
        functools.partial(_inproj_kernel, n_round=len(round_stacks)),
        grid=(m // tm,),
        in_specs=[pl.BlockSpec((tm, D_MODEL), row), _resident_weight(w)] + [tab_spec] * 6
        + round_in,
        out_specs=[pl.BlockSpec((tm, WIDTH_A), row)] * 3
        + [pl.BlockSpec((tm, WIDTH_B), row), pl.BlockSpec((tm, 2 * KV_WIDTH_B), row)]
        + round_out,
        out_shape=[out(WIDTH_A)] * 3 + [out(WIDTH_B), out(2 * KV_WIDTH_B)] + round_shape,
        compiler_params=_params("arbitrary"),
        name="inproj",
    )(x, w, *tabs_a, *tabs_b, *round_stacks)
    return outs[:5], outs[5:]


def _band_mask(max_dist):
    qi = lax.broadcasted_iota(jnp.int32, (BLOCK, 2 * BLOCK), 0) + BLOCK
    kj = lax.broadcasted_iota(jnp.int32, (BLOCK, 2 * BLOCK), 1)
    dist = qi - kj
    return (dist >= 0) & (dist <= max_dist)


def _softmax_block(q, k, v, bias):
    s = _dot_nt(q, k) + bias
    m = jnp.max(s, axis=-1, keepdims=True)
    p = jnp.exp2(s - m)
    l = jnp.sum(p, axis=-1, keepdims=True)
    return m, l, _dot(p.astype(BF16), v)


def _softmax_block_onto(q, k, v, bias, m_prev, l_prev, acc_prev):
    s = _dot_nt(q, k) + bias
    m = jnp.maximum(m_prev, jnp.max(s, axis=-1, keepdims=True))
    scale_prev = jnp.exp2(m_prev - m)
    p = jnp.exp2(s - jnp.concatenate([m] * (s.shape[1] // BLOCK), axis=1))
    l = scale_prev * l_prev + jnp.sum(p, axis=-1, keepdims=True)
    return m, l, scale_prev * acc_prev + _dot(p.astype(BF16), v)


def _piece(offset):
    return pl.ds(offset if isinstance(offset, int) else pl.multiple_of(offset, PIECE), PIECE)


def _gather(ref, offsets, cols):
    return jnp.concatenate([ref[_piece(o), cols] for o in offsets], axis=0)


def _scatter(ref, offsets, cols, val):
    for i, o in enumerate(offsets):
        ref[_piece(o), cols] = val[i * PIECE:(i + 1) * PIECE]


def _attn_a_kernel(perm_ref, q_ref, k_ref, v_ref, o_ref,
                   qt_ref, kt_ref, vt_ref, acc4_ref, m4_ref, l4_ref, acc16_ref, m16_ref, l16_ref,
                   bias_ref, *, heads):
    seq = q_ref.shape[0]
    n_groups = seq // GROUP
    perm = perm_ref[...]
    head_cols = [slice(h * HEAD_DIM_A, (h + 1) * HEAD_DIM_A) for h in range(heads)]
    all_lanes = slice(0, BLOCK)
    tile = lambda col: jnp.broadcast_to(col, (BLOCK, BLOCK))

    def i4_of(idx):
        return 64 * (idx >> 6) + 4 * (idx & 15) + ((idx >> 4) & 3)

    qi = lax.broadcasted_iota(jnp.int32, (BLOCK, 2 * BLOCK), 0)
    kj = lax.broadcasted_iota(jnp.int32, (BLOCK, 2 * BLOCK), 1)
    dist4 = BLOCK + i4_of(qi) - i4_of(kj)
    for i, band in enumerate((_band_mask(BLOCK), (dist4 >= 0) & (dist4 <= BLOCK))):
        bias_ref[i] = jnp.where(band, 0.0, NEG_INF)
        bias_ref[2 + i] = jnp.where(band & (kj >= BLOCK), 0.0, NEG_INF)

    for g in range(n_groups):
        rows = slice(g * GROUP, (g + 1) * GROUP)
        for src, dst in ((q_ref, qt_ref), (k_ref, kt_ref), (v_ref, vt_ref)):
            dst[rows, :] = _dot(perm, src[rows, :]).astype(BF16)

    def transposed_block(q_off, prev_off, band, acc_ref, m_ref, l_ref):
        if prev_off is None:
            k_off, bias = q_off + q_off, bias_ref[2 + band]
        else:
            k_off, bias = prev_off + q_off, bias_ref[band]
        for h, cols in enumerate(head_cols):
            m, l, acc = _softmax_block(_gather(qt_ref, q_off, cols), _gather(kt_ref, k_off, cols),
                                       _gather(vt_ref, k_off, cols), bias)
            _scatter(acc_ref, q_off, cols, acc)
            _scatter(m_ref.at[h], q_off, all_lanes, tile(m))
            _scatter(l_ref.at[h], q_off, all_lanes, tile(l))

    for r4 in range(4):
        for n in range(seq // 4 // BLOCK):
            q_off = [GROUP * (2 * n + g) + PIECE * (r4 + 4 * k)
                     for g in range(2) for k in range(4)]
            prev_off = [o - 2 * GROUP for o in q_off] if n else None
            transposed_block(q_off, prev_off, 1, acc4_ref, m4_ref, l4_ref)

    groups_per_block = BLOCK // PIECE
    for r in range(PIECE):
        for n in range(seq // PIECE // BLOCK):
            q_off = [GROUP * (groups_per_block * n + g) + PIECE * r
                     for g in range(groups_per_block)]
            prev_off = [o - GROUP * groups_per_block for o in q_off] if n else None
            transposed_block(q_off, prev_off, 0, acc16_ref, m16_ref, l16_ref)

    def dilation1(g, first):
        g = jnp.int32(g)
        rows = pl.ds(pl.multiple_of(g * GROUP, GROUP), GROUP)
        accs, stats = [], []
        for h, cols in enumerate(head_cols):
            m4, m16 = m4_ref[h, rows, :], m16_ref[h, rows, :]
            m = jnp.maximum(m4, m16).astype(BF16)
            w4, w16 = jnp.exp2(m4 - m.astype(F32)), jnp.exp2(m16 - m.astype(F32))
            l = w4 * l4_ref[h, rows, :] + w16 * l16_ref[h, rows, :]
            l_hi = l.astype(BF16)
            accs.append((w4 * acc4_ref[rows, cols] + w16 * acc16_ref[rows, cols]).astype(BF16))
            stats += [m, l_hi, (l - l_hi.astype(F32)).astype(BF16)]
        carry = _dot(perm, jnp.concatenate(accs + stats, axis=1))
        for j in range(GROUP // BLOCK):
            local = slice(j * BLOCK, (j + 1) * BLOCK)
            r0 = pl.multiple_of(g * GROUP + j * BLOCK, BLOCK)
            for h, cols in enumerate(head_cols):
                stats = heads * HEAD_DIM_A + 3 * BLOCK * h
                m_prev = carry[local, stats:stats + BLOCK]
                l_prev = (carry[local, stats + BLOCK:stats + 2 * BLOCK]
                          + carry[local, stats + 2 * BLOCK:stats + 3 * BLOCK])
                if first and j == 0:
                    p0, bias = (r0, r0), bias_ref[2]
                else:
                    p0 = pl.multiple_of(g * GROUP + (j - 1) * BLOCK, BLOCK)
                    p0, bias = (p0, r0), bias_ref[0]
                k2 = jnp.concatenate([k_ref[pl.ds(p, BLOCK), cols] for p in p0], axis=0)
                v2 = jnp.concatenate([v_ref[pl.ds(p, BLOCK), cols] for p in p0], axis=0)
                _, l, acc = _softmax_block_onto(q_ref[pl.ds(r0, BLOCK), cols], k2, v2, bias,
                                                m_prev, l_prev, carry[local, cols])
                o_ref[pl.ds(r0, BLOCK), cols] = (acc * (1.0 / l)).astype(BF16)

    def later_groups(t, carry):
        for i in range(GROUPS_PER_BODY):
            dilation1(1 + GROUPS_PER_BODY * t + i, False)
        return carry

    dilation1(0, True)
    lax.fori_loop(0, (n_groups - 1) // GROUPS_PER_BODY, later_groups, 0)


def _group_transpose_matrix():
    idx = jnp.arange(GROUP)
    return (idx[:, None] == PIECE * (idx[None, :] % PIECE) + idx[None, :] // PIECE).astype(BF16)


def _attn_a(qa, ka, va, batch, seq, heads=2):
    assert (seq // GROUP - 1) % GROUPS_PER_BODY == 0
    width = heads * HEAD_DIM_A
    view = lambda t: t.reshape(batch, seq, WIDTH_A)
    blk = pl.BlockSpec((None, seq, width), lambda b, h: (b, 0, h))
    stat = pltpu.VMEM((heads, seq, BLOCK), F32)
    o = pl.pallas_call(
        functools.partial(_attn_a_kernel, heads=heads),
        grid=(batch, N_HEADS_A // heads),
        in_specs=[pl.BlockSpec((GROUP, GROUP), lambda b, h: (0, 0)), blk, blk, blk],
        out_specs=blk,
        out_shape=jax.ShapeDtypeStruct((batch, seq, WIDTH_A), BF16),
        scratch_shapes=[pltpu.VMEM((seq, width), BF16)] * 3
        + [pltpu.VMEM((seq, width), F32), stat, stat] * 2
        + [pltpu.VMEM((4, BLOCK, 2 * BLOCK), F32)],
        compiler_params=_params("arbitrary", "arbitrary"),
        name="attn_a",
    )(_group_transpose_matrix(), view(qa), view(ka), view(va))
    return o.reshape(batch * seq, WIDTH_A)


def _attn_b_kernel(sink_ref, q_ref, kvp_ref, kv_ref, o_ref, *, rows):
    stack = lambda t: jnp.concatenate([t, t], axis=0)
    qi = lax.broadcasted_iota(jnp.int32, (BLOCK, BLOCK), 0)
    kj = lax.broadcasted_iota(jnp.int32, (BLOCK, BLOCK), 1)
    from_prev = stack(kj > qi)
    no_prev = jnp.where(pl.program_id(1) == 0, NEG_INF, 0.0)
    low = kj < HEAD_DIM_B
    low2 = stack(low)
    row2 = lax.broadcasted_iota(jnp.int32, (2 * BLOCK, 1), 0)
    heads_per_group = N_Q_HEADS_B // N_KV_HEADS_B
    for j in range(rows // BLOCK):
        r0 = j * BLOCK
        if j == 0:
            kv2 = jnp.concatenate([kvp_ref[...], kv_ref[:BLOCK, :]], axis=0)
        else:
            kv2 = kv_ref[r0 - BLOCK:r0 + BLOCK, :]
        kv2 = kv2.astype(F32)
        k2, v2 = kv2[:, :KV_WIDTH_B], kv2[:, KV_WIDTH_B:]
        k2r, v2r = pltpu.roll(k2, HEAD_DIM_B, 1), pltpu.roll(v2, HEAD_DIM_B, 1)
        kdup = [jnp.where(low2, k2, k2r).astype(BF16), jnp.where(low2, k2r, k2).astype(BF16)]
        vdup = [jnp.where(low2, v2, v2r).astype(BF16), jnp.where(low2, v2r, v2).astype(BF16)]
        for t in range(WIDTH_B // BLOCK):
            g = (2 * t) // heads_per_group
            cols = slice(t * BLOCK, (t + 1) * BLOCK)
            q = q_ref[r0:r0 + BLOCK, cols]
            zero = jnp.zeros_like(q)
            q2 = jnp.concatenate([jnp.where(low, q, zero), jnp.where(low, zero, q)], axis=0)
            s2 = _dot_nt(q2, kdup[g])
            s_prev = s2[:, :BLOCK] + no_prev if j == 0 else s2[:, :BLOCK]
            s = jnp.where(from_prev, s_prev, s2[:, BLOCK:])
            m = jnp.max(s, axis=-1, keepdims=True)
            p = jnp.exp2(s - m)
            l = jnp.sum(p, axis=-1, keepdims=True)
            p2 = jnp.concatenate([jnp.where(from_prev, p, 0.0), jnp.where(from_prev, 0.0, p)],
                                 axis=1)
            acc = _dot(p2.astype(BF16), vdup[g])
            sink = jnp.where(row2 < BLOCK, sink_ref[2 * t], sink_ref[2 * t + 1]) * LOG2E
            y = acc * (1.0 / (l + jnp.exp2(sink - m)))
            o_ref[r0:r0 + BLOCK, cols] = jnp.where(low, y[:BLOCK], y[BLOCK:]).astype(BF16)


def _attn_b(sinks, qb, kvb, batch, seq, rows=1024):
    nblk = seq // rows
    prev_per_blk = rows // BLOCK
    view = lambda t: t.reshape(batch, seq, t.shape[-1])
    cur = lambda b, n: (b, n, 0)
    prev = lambda b, n: (b, jnp.maximum(n * prev_per_blk - 1, 0), 0)
    o = pl.pallas_call(
        functools.partial(_attn_b_kernel, rows=rows),
        grid=(batch, nblk),
        in_specs=[pl.BlockSpec(memory_space=pltpu.SMEM),
                  pl.BlockSpec((None, rows, WIDTH_B), cur),
                  pl.BlockSpec((None, BLOCK, 2 * KV_WIDTH_B), prev),
                  pl.BlockSpec((None, rows, 2 * KV_WIDTH_B), cur)],
        out_specs=pl.BlockSpec((None, rows, WIDTH_B), cur),
        out_shape=jax.ShapeDtypeStruct((batch, seq, WIDTH_B), BF16),
        compiler_params=_params("arbitrary", "arbitrary"),
        name="attn_b",
    )(sinks, view(qb), view(kvb), view(kvb))
    return o.reshape(batch * seq, WIDTH_B)


def _rms_gain(y, gain):
    return y * lax.rsqrt(jnp.mean(y * y, axis=-1, keepdims=True) + RMS_EPS) * gain


def _outproj_kernel(*refs, n_round):
    ya_ref, yb_ref, x_ref, w_ref, gna_ref, gnb_ref, g_ref, b_ref = refs[:8]
    out_ref = refs[8 + n_round]
    _round_slices(refs[8:8 + n_round], refs[9 + n_round:])
    ya = _rms_gain(ya_ref[...].astype(F32), gna_ref[...]).astype(BF16)
    yb = _rms_gain(yb_ref[...].astype(F32), gnb_ref[...]).astype(BF16)
    y = _dot(jnp.concatenate([ya, yb], axis=1), w_ref[...])
    out_ref[...] = _layernorm(ALPHA * x_ref[...] + y, g_ref[...], b_ref[...])


def _layer_vec(width, layer):
    return pl.BlockSpec((None, 1, width), lambda *_: (layer, 0, 0))


def _outproj(ya, yb, x, w, gna, gnb, g, b, layer, round_stacks=(), tm=512):
    m = x.shape[0]
    row = lambda i: (i, 0)
    round_in, round_out, round_shape = _round_job(
        round_stacks, [layer + 1] * len(round_stacks), m // tm)
    outs = pl.pallas_call(
        functools.partial(_outproj_kernel, n_round=len(round_stacks)),
        grid=(m // tm,),
        in_specs=[pl.BlockSpec((tm, WIDTH_A), row), pl.BlockSpec((tm, WIDTH_B), row),
                  pl.BlockSpec((tm, D_MODEL), row),
                  _resident_weight(w),
                  _layer_vec(WIDTH_A, layer), _layer_vec(WIDTH_B, layer),
                  _layer_vec(D_MODEL, layer), _layer_vec(D_MODEL, layer)] + round_in,
        out_specs=[pl.BlockSpec((tm, D_MODEL), row)] + round_out,
        out_shape=[jax.ShapeDtypeStruct((m, D_MODEL), F32)] + round_shape,
        compiler_params=_params("arbitrary"),
        name="outproj",
    )(ya, yb, x, w, gna, gnb, g, b, *round_stacks)
    return outs[0], outs[1:]


def _memkv_kernel(mem_ref, w_ref, kv_ref):
    kv_ref[...] = _dot(mem_ref[...].astype(BF16), w_ref[...].astype(BF16)).astype(BF16)


def _memkv(mem, w_mkv, tn=1024):
    rows = mem.shape[0]
    return pl.pallas_call(
        _memkv_kernel,
        grid=(DEPTH, 2 * D_MODEL // tn),
        in_specs=[pl.BlockSpec((rows, D_MODEL), lambda i, j: (0, 0)),
                  pl.BlockSpec((None, D_MODEL, tn), lambda i, j: (i, 0, j))],
        out_specs=pl.BlockSpec((None, rows, tn), lambda i, j: (i, 0, j)),
        out_shape=jax.ShapeDtypeStruct((DEPTH, rows, 2 * D_MODEL), BF16),
        compiler_params=_params("arbitrary", "arbitrary"),
        name="memkv",
    )(mem, w_mkv)


def _memattn_kernel(x_ref, wq_ref, k_ref, v_ref, wo_ref, g_ref, b_ref, out_ref):
    x = x_ref[...]
    q = (_dot(x.astype(BF16), wq_ref[...]) * MEM_HEAD_DIM ** -0.5).astype(BF16)
    heads = []
    for h in range(N_MEM_HEADS):
        cols = slice(h * MEM_HEAD_DIM, (h + 1) * MEM_HEAD_DIM)
        s = _dot_nt(q[:, cols], k_ref[:, cols])
        m = jnp.max(s, axis=-1, keepdims=True)
        p = jnp.exp(s - m)
        l = jnp.sum(p, axis=-1, keepdims=True)
        o = _dot(p.astype(BF16), v_ref[:, cols])
        heads.append((o * (1.0 / l)).astype(BF16))
    y = _dot(jnp.concatenate(heads, axis=1), wo_ref[...])
    out_ref[...] = _layernorm(ALPHA * x + y, g_ref[...], b_ref[...])


def _memattn(x, wq, kv, wo, g, b, batch, seq, layer, tm=512):
    n_mem = kv.shape[1] // batch
    per_batch = seq // tm
    row = lambda i: (i, 0)
    weight = _resident_weight(wq)
    kv_spec = lambda half: pl.BlockSpec((None, n_mem, D_MODEL),
                                        lambda i: (layer, i // per_batch, half))
    return pl.pallas_call(
        _memattn_kernel,
        grid=(batch * seq // tm,),
        in_specs=[pl.BlockSpec((tm, D_MODEL), row), weight, kv_spec(0), kv_spec(1), weight,
                  _layer_vec(D_MODEL, layer), _layer_vec(D_MODEL, layer)],
        out_specs=pl.BlockSpec((tm, D_MODEL), row),
        out_shape=jax.ShapeDtypeStruct((batch * seq, D_MODEL), F32),
        compiler_params=_params("arbitrary"),
        name="memattn",
    )(x, wq, kv, kv, wo, g, b)


def _mlp_kernel(x_ref, wu_ref, wd_ref, g_ref, b_ref, out_ref):
    j = pl.program_id(1)

    @pl.when(j == 0)
    def _():
        out_ref[...] = ALPHA * x_ref[...]

    h = jnp.maximum(_dot(x_ref[...].astype(BF16), wu_ref[...].astype(BF16)), 0.0)
    out_ref[...] += _dot((h * h).astype(BF16), wd_ref[...].astype(BF16))

    @pl.when(j == pl.num_programs(1) - 1)
    def _():
        out_ref[...] = _layernorm(out_ref[...], g_ref[...], b_ref[...])


def _mlp(x, wu, wd, g, b, layer, tm=1024, tf=512):
    m = x.shape[0]
    row = lambda i, j: (i, 0)
    return pl.pallas_call(
        _mlp_kernel,
        grid=(m // tm, D_FF // tf),
        in_specs=[pl.BlockSpec((tm, D_MODEL), row),
                  pl.BlockSpec((None, D_MODEL, tf), lambda i, j: (layer, 0, j)),
                  pl.BlockSpec((None, tf, D_MODEL), lambda i, j: (layer, j, 0)),
                  _layer_vec(D_MODEL, layer), _layer_vec(D_MODEL, layer)],
        out_specs=pl.BlockSpec((tm, D_MODEL), row),
        out_shape=jax.ShapeDtypeStruct((m, D_MODEL), F32),
        compiler_params=_params("arbitrary", "arbitrary"),
        name="mlp",
    )(x, wu, wd, g, b)


def _rope_lane_tables(positions, head_dim, rot_dim):
    half = rot_dim // 2
    inv_freq = ROPE_THETA ** (-jnp.arange(0, rot_dim, 2, dtype=F32) / rot_dim)
    ang = positions.astype(F32).reshape(-1, 1) * inv_freq
    cos, sin = jnp.cos(ang), jnp.sin(ang)
    rows = ang.shape[0]
    pad = jnp.zeros((rows, head_dim - rot_dim), F32)
    zero = jnp.zeros((rows, half), F32)
    reps = BLOCK // head_dim
    c = jnp.tile(jnp.concatenate([cos, cos, pad + 1.0], axis=1), (1, reps))
    s_up = jnp.tile(jnp.concatenate([zero, sin, pad], axis=1), (1, reps))
    s_dn = jnp.tile(jnp.concatenate([-sin, zero, pad], axis=1), (1, reps))
    return c, s_up, s_dn


def kernel(x, mem, positions, w_in, gn_a, gn_b, sinks, w_out, ln_mix_g, ln_mix_b, w_mq, w_mkv,
           w_mo, ln_mem_g, ln_mem_b, w_up, w_down, ln_ff_g, ln_ff_b):
    batch, seq, _ = x.shape
    tabs_a = _rope_lane_tables(positions, HEAD_DIM_A, ROT_DIM_A)
    tabs_b = _rope_lane_tables(positions, HEAD_DIM_B, ROT_DIM_B)
    (w_in_i,) = _round_layer((w_in,), 0)
    later = (w_out, w_mq, w_mo)
    gn_a, gn_b, ln_mix_g, ln_mix_b, ln_mem_g, ln_mem_b, ln_ff_g, ln_ff_b = (
        v.reshape(DEPTH, 1, -1)
        for v in (gn_a, gn_b, ln_mix_g, ln_mix_b, ln_mem_g, ln_mem_b, ln_ff_g, ln_ff_b))
    kv = _memkv(mem.reshape(-1, D_MODEL), w_mkv)
    h = x.reshape(batch * seq, D_MODEL)
    for i in range(DEPTH):
        more = i + 1 < DEPTH
        jobs = [(w_in, i + 1)] * more + [(w, 0) for w in later] * (i == 0)
        (qa, ka, va, qb, kvb), rounded = _inproj(
            h, w_in_i, tabs_a, tabs_b, [w for w, _ in jobs], [l for _, l in jobs])
        next_in = rounded[:1] if more else ()
        if i == 0:
            w_out_i, w_mq_i, w_mo_i = rounded[-3:]
        ya = _attn_a(qa, ka, va, batch, seq)
        yb = _attn_b(sinks[i], qb, kvb, batch, seq)
        h, next_rest = _outproj(ya, yb, h, w_out_i, gn_a, gn_b, ln_mix_g, ln_mix_b, i,
                                later if more else ())
        h = _memattn(h, w_mq_i, kv, w_mo_i, ln_mem_g, ln_mem_b, batch, seq, i)
        h = _mlp(h, w_up, w_down, ln_ff_g, ln_ff_b, i)
        if more:
            (w_in_i,), (w_out_i, w_mq_i, w_mo_i) = next_in, next_rest
    return h.reshape(batch, seq, D_MODEL)
```

```python
import functools
import math

import jax
import jax.numpy as jnp
from jax import lax
from jax.experimental import pallas as pl
from jax.experimental.pallas import tpu as pltpu

D_MODEL = 2048
DEPTH = 4
N_MEM_HEADS = 4
MEM_HEAD_DIM = D_MODEL // N_MEM_HEADS
HEAD_DIM_A = 128
N_HEADS_A = 8
WIDTH_A = N_HEADS_A * HEAD_DIM_A
HEAD_DIM_B = 64
N_Q_HEADS_B = 16
N_KV_HEADS_B = 2
WIDTH_B = N_Q_HEADS_B * HEAD_DIM_B
KV_WIDTH_B = N_KV_HEADS_B * HEAD_DIM_B
ROPE_THETA = 500000.0
ROT_DIM_A = HEAD_DIM_A // 4
ROT_DIM_B = HEAD_DIM_B // 4
D_FF = 4 * D_MODEL
BLOCK = 128
ALPHA = (2 * DEPTH) ** 0.25
LN_EPS = 1e-5
RMS_EPS = 1e-6
NEG_INF = -1e30
LOG2E = math.log2(math.e)

PIECE = 16
GROUP = PIECE * PIECE
GROUPS_PER_BODY = 5

BF16 = jnp.bfloat16
F32 = jnp.float32

VMEM_LIMIT_BYTES = 58 * 1024 * 1024


def _params(*semantics):
    return pltpu.CompilerParams(dimension_semantics=semantics,
                                vmem_limit_bytes=VMEM_LIMIT_BYTES)


def _single_buffered(block_shape, index_map):
    return pl.BlockSpec(block_shape, index_map, pipeline_mode=pl.Buffered(1))


def _resident_weight(w):
    return _single_buffered((None,) + w.shape[1:], lambda *_: (0, 0, 0))


def _round_job(stacks, layers, steps):
    rows = [w.shape[1] // steps for w in stacks]
    assert all(r * steps == w.shape[1] and r % PIECE == 0 for r, w in zip(rows, stacks))
    blocks = [(None, r, w.shape[2]) for r, w in zip(rows, stacks)]
    return ([pl.BlockSpec(blk, lambda i, layer=layer: (layer, i, 0))
             for blk, layer in zip(blocks, layers)],
            [pl.BlockSpec(blk, lambda i: (0, i, 0)) for blk in blocks],
            [jax.ShapeDtypeStruct((1,) + w.shape[1:], BF16) for w in stacks])


def _round_slices(src_refs, dst_refs):
    for src, dst in zip(src_refs, dst_refs):
        dst[...] = src[...].astype(BF16)


def _round_kernel(*refs):
    _round_slices(refs[:len(refs) // 2], refs[len(refs) // 2:])


def _round_layer(stacks, layer, steps=8):
    in_specs, out_specs, out_shape = _round_job(stacks, [layer] * len(stacks), steps)
    return pl.pallas_call(
        _round_kernel, grid=(steps,), in_specs=in_specs, out_specs=out_specs,
        out_shape=out_shape, compiler_params=_params("arbitrary"), name="round_weights",
    )(*stacks)


def _layernorm(z, g, b):
    mu = jnp.mean(z, axis=-1, keepdims=True)
    zc = z - mu
    var = jnp.mean(zc * zc, axis=-1, keepdims=True)
    return zc * lax.rsqrt(var + LN_EPS) * g + b


def _dot_nt(a, b):
    return lax.dot_general(a, b, (((1,), (1,)), ((), ())), preferred_element_type=F32)


def _dot(a, b):
    return jnp.dot(a, b, preferred_element_type=F32)


def _rope(t, c, s_up, s_dn, half):
    return (t * c + pltpu.roll(t, half, 1) * s_up
            + pltpu.roll(t, BLOCK - half, 1) * s_dn)


def _inproj_kernel(*refs, n_round):
    x_ref, w_ref, ca_ref, ua_ref, da_ref, cb_ref, ub_ref, db_ref = refs[:8]
    qa_ref, ka_ref, va_ref, qb_ref, kvb_ref = refs[8 + n_round:13 + n_round]
    _round_slices(refs[8:8 + n_round], refs[13 + n_round:])
    x = x_ref[...].astype(BF16)
    ca, ua, da = ca_ref[...], ua_ref[...], da_ref[...]
    cb, ub, db = cb_ref[...], ub_ref[...], db_ref[...]
    scale_a = HEAD_DIM_A ** -0.5 * LOG2E
    scale_b = HEAD_DIM_B ** -0.5 * LOG2E
    chunk = 4 * BLOCK

    def proj(col, width):
        return _dot(x, w_ref[:, col:col + width])

    for c0 in range(0, WIDTH_A, chunk):
        hq = proj(c0, chunk)
        hk = proj(WIDTH_A + c0, chunk)
        for j in range(0, chunk, BLOCK):
            sl = slice(c0 + j, c0 + j + BLOCK)
            qa_ref[:, sl] = (_rope(hq[:, j:j + BLOCK], ca, ua, da, ROT_DIM_A // 2)
                             * scale_a).astype(BF16)
            ka_ref[:, sl] = _rope(hk[:, j:j + BLOCK], ca, ua, da,
                                  ROT_DIM_A // 2).astype(BF16)
        va_ref[:, c0:c0 + chunk] = proj(2 * WIDTH_A + c0, chunk).astype(BF16)
    for c0 in range(0, WIDTH_B, chunk):
        hq = proj(3 * WIDTH_A + c0, chunk)
        for j in range(0, chunk, BLOCK):
            qb_ref[:, c0 + j:c0 + j + BLOCK] = (
                _rope(hq[:, j:j + BLOCK], cb, ub, db, ROT_DIM_B // 2) * scale_b).astype(BF16)
    hkv = proj(3 * WIDTH_A + WIDTH_B, 2 * KV_WIDTH_B)
    kvb_ref[:, :KV_WIDTH_B] = _rope(hkv[:, :KV_WIDTH_B], cb, ub, db,
                                    ROT_DIM_B // 2).astype(BF16)
    kvb_ref[:, KV_WIDTH_B:] = hkv[:, KV_WIDTH_B:].astype(BF16)


def _inproj(x, w, tabs_a, tabs_b, round_stacks=(), round_layers=(), tm=512):
    m = x.shape[0]
    row = lambda i: (i, 0)
    tab_spec = pl.BlockSpec((tm, BLOCK), row)
    out = lambda width: jax.ShapeDtypeStruct((m, width), BF16)
    round_in, round_out, round_shape = _round_job(round_stacks, round_layers, m // tm)
    outs = pl.pallas_call(
        functools.partial(_inproj_kernel, n_round=len(round_stacks)),
        grid=(m // tm,),
        in_specs=[pl.BlockSpec((tm, D_MODEL), row), _resident_weight(w)] + [tab_spec] * 6
        + round_in,
        out_specs=[pl.BlockSpec((tm, WIDTH_A), row)] * 3
        + [pl.BlockSpec((tm, WIDTH_B), row), pl.BlockSpec((tm, 2 * KV_WIDTH_B), row)]
        + round_out,
        out_shape=[out(WIDTH_A)] * 3 + [out(WIDTH_B), out(2 * KV_WIDTH_B)] + round_shape,
        compiler_params=_params("arbitrary"),
        name="inproj",
    )(x, w, *tabs_a, *tabs_b, *round_stacks)
    return outs[:5], outs[5:]


def _band_mask(max_dist):
    qi = lax.broadcasted_iota(jnp.int32, (BLOCK, 2 * BLOCK), 0) + BLOCK
    kj = lax.broadcasted_iota(jnp.int32, (BLOCK, 2 * BLOCK), 1)
    dist = qi - kj
    return (dist >= 0) & (dist <= max_dist)


def _softmax_block(q, k, v, bias):
    s = _dot_nt(q, k) + bias
    m = jnp.max(s, axis=-1, keepdims=True)
    p = jnp.exp2(s - m)
    l = jnp.sum(p, axis=-1, keepdims=True)
    return m, l, _dot(p.astype(BF16), v)


def _softmax_block_onto(q, k, v, bias, m_prev, l_prev, acc_prev):
    s = _dot_nt(q, k) + bias
    m = jnp.maximum(m_prev, jnp.max(s, axis=-1, keepdims=True))
    scale_prev = jnp.exp2(m_prev - m)
    p = jnp.exp2(s - jnp.concatenate([m] * (s.shape[1] // BLOCK), axis=1))
    l = scale_prev * l_prev + jnp.sum(p, axis=-1, keepdims=True)
    return m, l, scale_prev * acc_prev + _dot(p.astype(BF16), v)


def _piece(offset):
    return pl.ds(offset if isinstance(offset, int) else pl.multiple_of(offset, PIECE), PIECE)


def _gather(ref, offsets, cols):
    return jnp.concatenate([ref[_piece(o), cols] for o in offsets], axis=0)


def _scatter(ref, offsets, cols, val):
    for i, o in enumerate(offsets):
        ref[_piece(o), cols] = val[i * PIECE:(i + 1) * PIECE]


def _attn_a_kernel(perm_ref, q_ref, k_ref, v_ref, o_ref,
                   qt_ref, kt_ref, vt_ref, acc4_ref, m4_ref, l4_ref, acc16_ref, m16_ref, l16_ref,
                   bias_ref, *, heads):
    seq = q_ref.shape[0]
    n_groups = seq // GROUP
    perm = perm_ref[...]
    head_cols = [slice(h * HEAD_DIM_A, (h + 1) * HEAD_DIM_A) for h in range(heads)]
    all_lanes = slice(0, BLOCK)
    tile = lambda col: jnp.broadcast_to(col, (BLOCK, BLOCK))

    def i4_of(idx):
        return 64 * (idx >> 6) + 4 * (idx & 15) + ((idx >> 4) & 3)

    qi = lax.broadcasted_iota(jnp.int32, (BLOCK, 2 * BLOCK), 0)
    kj = lax.broadcasted_iota(jnp.int32, (BLOCK, 2 * BLOCK), 1)
    dist4 = BLOCK + i4_of(qi) - i4_of(kj)
    for i, band in enumerate((_band_mask(BLOCK), (dist4 >= 0) & (dist4 <= BLOCK))):
        bias_ref[i] = jnp.where(band, 0.0, NEG_INF)
        bias_ref[2 + i] = jnp.where(band & (kj >= BLOCK), 0.0, NEG_INF)

    for g in range(n_groups):
        rows = slice(g * GROUP, (g + 1) * GROUP)
        for src, dst in ((q_ref, qt_ref), (k_ref, kt_ref), (v_ref, vt_ref)):
            dst[rows, :] = _dot(perm, src[rows, :]).astype(BF16)

    def transposed_block(q_off, prev_off, band, acc_ref, m_ref, l_ref):
        if prev_off is None:
            k_off, bias = q_off + q_off, bias_ref[2 + band]
        else:
            k_off, bias = prev_off + q_off, bias_ref[band]
        for h, cols in enumerate(head_cols):
            m, l, acc = _softmax_block(_gather(qt_ref, q_off, cols), _gather(kt_ref, k_off, cols),
                                       _gather(vt_ref, k_off, cols), bias)
            _scatter(acc_ref, q_off, cols, acc)
            _scatter(m_ref.at[h], q_off, all_lanes, tile(m))
            _scatter(l_ref.at[h], q_off, all_lanes, tile(l))

    for r4 in range(4):
        for n in range(seq // 4 // BLOCK):
            q_off = [GROUP * (2 * n + g) + PIECE * (r4 + 4 * k)
                     for g in range(2) for k in range(4)]
            prev_off = [o - 2 * GROUP for o in q_off] if n else None
            transposed_block(q_off, prev_off, 1, acc4_ref, m4_ref, l4_ref)

    groups_per_block = BLOCK // PIECE
    for r in range(PIECE):
        for n in range(seq // PIECE // BLOCK):
            q_off = [GROUP * (groups_per_block * n + g) + PIECE * r
                     for g in range(groups_per_block)]
            prev_off = [o - GROUP * groups_per_block for o in q_off] if n else None
            transposed_block(q_off, prev_off, 0, acc16_ref, m16_ref, l16_ref)

    def dilation1(g, first):
        g = jnp.int32(g)
        rows = pl.ds(pl.multiple_of(g * GROUP, GROUP), GROUP)
        accs, stats = [], []
        for h, cols in enumerate(head_cols):
            m4, m16 = m4_ref[h, rows, :], m16_ref[h, rows, :]
            m = jnp.maximum(m4, m16)
            m = (m + jnp.abs(m) * 2.0 ** -7).astype(BF16)
            w4, w16 = jnp.exp2(m4 - m.astype(F32)), jnp.exp2(m16 - m.astype(F32))
            l = w4 * l4_ref[h, rows, :] + w16 * l16_ref[h, rows, :]
            l_hi = l.astype(BF16)
            accs.append((w4 * acc4_ref[rows, cols] + w16 * acc16_ref[rows, cols]).astype(BF16))
            stats += [m, l_hi, (l - l_hi.astype(F32)).astype(BF16)]
        carry = _dot(perm, jnp.concatenate(accs + stats, axis=1))
        for j in range(GROUP // BLOCK):
            local = slice(j * BLOCK, (j + 1) * BLOCK)
            r0 = pl.multiple_of(g * GROUP + j * BLOCK, BLOCK)
            for h, cols in enumerate(head_cols):
                stats = heads * HEAD_DIM_A + 3 * BLOCK * h
                m_prev = carry[local, stats:stats + BLOCK]
                l_prev = (carry[local, stats + BLOCK:stats + 2 * BLOCK]
                          + carry[local, stats + 2 * BLOCK:stats + 3 * BLOCK])
                if first and j == 0:
                    p0, bias = (r0, r0), bias_ref[2]
                else:
                    p0 = pl.multiple_of(g * GROUP + (j - 1) * BLOCK, BLOCK)
                    p0, bias = (p0, r0), bias_ref[0]
                k2 = jnp.concatenate([k_ref[pl.ds(p, BLOCK), cols] for p in p0], axis=0)
                v2 = jnp.concatenate([v_ref[pl.ds(p, BLOCK), cols] for p in p0], axis=0)
                _, l, acc = _softmax_block_onto(q_ref[pl.ds(r0, BLOCK), cols], k2, v2, bias,
                                                m_prev, l_prev, carry[local, cols])
                o_ref[pl.ds(r0, BLOCK), cols] = (acc * (1.0 / l)).astype(BF16)

    def later_groups(t, carry):
        for i in range(GROUPS_PER_BODY):
            dilation1(1 + GROUPS_PER_BODY * t + i, False)
        return carry

    dilation1(0, True)
    lax.fori_loop(0, (n_groups - 1) // GROUPS_PER_BODY, later_groups, 0)


def _group_transpose_matrix():
    idx = jnp.arange(GROUP)
    return (idx[:, None] == PIECE * (idx[None, :] % PIECE) + idx[None, :] // PIECE).astype(BF16)


def _attn_a(qa, ka, va, batch, seq, heads=2):
    assert (seq // GROUP - 1) % GROUPS_PER_BODY == 0
    width = heads * HEAD_DIM_A
    view = lambda t: t.reshape(batch, seq, WIDTH_A)
    blk = pl.BlockSpec((None, seq, width), lambda b, h: (b, 0, h))
    stat = pltpu.VMEM((heads, seq, BLOCK), F32)
    o = pl.pallas_call(
        functools.partial(_attn_a_kernel, heads=heads),
        grid=(batch, N_HEADS_A // heads),
        in_specs=[pl.BlockSpec((GROUP, GROUP), lambda b, h: (0, 0)), blk, blk, blk],
        out_specs=blk,
        out_shape=jax.ShapeDtypeStruct((batch, seq, WIDTH_A), BF16),
        scratch_shapes=[pltpu.VMEM((seq, width), BF16)] * 3
        + [pltpu.VMEM((seq, width), F32), stat, stat] * 2
        + [pltpu.VMEM((4, BLOCK, 2 * BLOCK), F32)],
        compiler_params=_params("arbitrary", "arbitrary"),
        name="attn_a",
    )(_group_transpose_matrix(), view(qa), view(ka), view(va))
    return o.reshape(batch * seq, WIDTH_A)


def _attn_b_kernel(sink_ref, q_ref, kvp_ref, kv_ref, o_ref, *, rows):
    stack = lambda t: jnp.concatenate([t, t], axis=0)
    qi = lax.broadcasted_iota(jnp.int32, (BLOCK, BLOCK), 0)
    kj = lax.broadcasted_iota(jnp.int32, (BLOCK, BLOCK), 1)
    from_prev = stack(kj > qi)
    no_prev = jnp.where(pl.program_id(1) == 0, NEG_INF, 0.0)
    low = kj < HEAD_DIM_B
    low2 = stack(low)
    row2 = lax.broadcasted_iota(jnp.int32, (2 * BLOCK, 1), 0)
    heads_per_group = N_Q_HEADS_B // N_KV_HEADS_B
    for j in range(rows // BLOCK):
        r0 = j * BLOCK
        if j == 0:
            kv2 = jnp.concatenate([kvp_ref[...], kv_ref[:BLOCK, :]], axis=0)
        else:
            kv2 = kv_ref[r0 - BLOCK:r0 + BLOCK, :]
        kv2 = kv2.astype(F32)
        k2, v2 = kv2[:, :KV_WIDTH_B], kv2[:, KV_WIDTH_B:]
        k2r, v2r = pltpu.roll(k2, HEAD_DIM_B, 1), pltpu.roll(v2, HEAD_DIM_B, 1)
        kdup = [jnp.where(low2, k2, k2r).astype(BF16), jnp.where(low2, k2r, k2).astype(BF16)]
        vdup = [jnp.where(low2, v2, v2r).astype(BF16), jnp.where(low2, v2r, v2).astype(BF16)]
        for t in range(WIDTH_B // BLOCK):
            g = (2 * t) // heads_per_group
            cols = slice(t * BLOCK, (t + 1) * BLOCK)
            q = q_ref[r0:r0 + BLOCK, cols]
            zero = jnp.zeros_like(q)
            q2 = jnp.concatenate([jnp.where(low, q, zero), jnp.where(low, zero, q)], axis=0)
            s2 = _dot_nt(q2, kdup[g])
            s_prev = s2[:, :BLOCK] + no_prev if j == 0 else s2[:, :BLOCK]
            s = jnp.where(from_prev, s_prev, s2[:, BLOCK:])
            m = jnp.max(s, axis=-1, keepdims=True)
            p = jnp.exp2(s - m)
            l = jnp.sum(p, axis=-1, keepdims=True)
            p2 = jnp.concatenate([jnp.where(from_prev, p, 0.0), jnp.where(from_prev, 0.0, p)],
                                 axis=1)
            acc = _dot(p2.astype(BF16), vdup[g])
            sink = jnp.where(row2 < BLOCK, sink_ref[2 * t], sink_ref[2 * t + 1]) * LOG2E
            y = acc * (1.0 / (l + jnp.exp2(sink - m)))
            o_ref[r0:r0 + BLOCK, cols] = jnp.where(low, y[:BLOCK], y[BLOCK:]).astype(BF16)


def _attn_b(sinks, qb, kvb, batch, seq, rows=1024):
    nblk = seq // rows
    prev_per_blk = rows // BLOCK
    view = lambda t: t.reshape(batch, seq, t.shape[-1])
    cur = lambda b, n: (b, n, 0)
    prev = lambda b, n: (b, jnp.maximum(n * prev_per_blk - 1, 0), 0)
    o = pl.pallas_call(
        functools.partial(_attn_b_kernel, rows=rows),
        grid=(batch, nblk),
        in_specs=[pl.BlockSpec(memory_space=pltpu.SMEM),
                  pl.BlockSpec((None, rows, WIDTH_B), cur),
                  pl.BlockSpec((None, BLOCK, 2 * KV_WIDTH_B), prev),
                  pl.BlockSpec((None, rows, 2 * KV_WIDTH_B), cur)],
        out_specs=pl.BlockSpec((None, rows, WIDTH_B), cur),
        out_shape=jax.ShapeDtypeStruct((batch, seq, WIDTH_B), BF16),
        compiler_params=_params("arbitrary", "arbitrary"),
        name="attn_b",
    )(sinks, view(qb), view(kvb), view(kvb))
    return o.reshape(batch * seq, WIDTH_B)


def _rms_gain(y, gain):
    return y * lax.rsqrt(jnp.mean(y * y, axis=-1, keepdims=True) + RMS_EPS) * gain


def _outproj_kernel(*refs, n_round):
    ya_ref, yb_ref, x_ref, w_ref, gna_ref, gnb_ref, g_ref, b_ref = refs[:8]
    out_ref = refs[8 + n_round]
    _round_slices(refs[8:8 + n_round], refs[9 + n_round:])
    ya = _rms_gain(ya_ref[...].astype(F32), gna_ref[...]).astype(BF16)
    yb = _rms_gain(yb_ref[...].astype(F32), gnb_ref[...]).astype(BF16)
    y = _dot(jnp.concatenate([ya, yb], axis=1), w_ref[...])
    out_ref[...] = _layernorm(ALPHA * x_ref[...] + y, g_ref[...], b_ref[...])


def _layer_vec(width, layer):
    return pl.BlockSpec((None, 1, width), lambda *_: (layer, 0, 0))


def _outproj(ya, yb, x, w, gna, gnb, g, b, layer, round_stacks=(), tm=512):
    m = x.shape[0]
    row = lambda i: (i, 0)
    round_in, round_out, round_shape = _round_job(
        round_stacks, [layer + 1] * len(round_stacks), m // tm)
    outs = pl.pallas_call(
        functools.partial(_outproj_kernel, n_round=len(round_stacks)),
        grid=(m // tm,),
        in_specs=[pl.BlockSpec((tm, WIDTH_A), row), pl.BlockSpec((tm, WIDTH_B), row),
                  pl.BlockSpec((tm, D_MODEL), row),
                  _resident_weight(w),
                  _layer_vec(WIDTH_A, layer), _layer_vec(WIDTH_B, layer),
                  _layer_vec(D_MODEL, layer), _layer_vec(D_MODEL, layer)] + round_in,
        out_specs=[pl.BlockSpec((tm, D_MODEL), row)] + round_out,
        out_shape=[jax.ShapeDtypeStruct((m, D_MODEL), F32)] + round_shape,
        compiler_params=_params("arbitrary"),
        name="outproj",
    )(ya, yb, x, w, gna, gnb, g, b, *round_stacks)
    return outs[0], outs[1:]


def _memkv_kernel(mem_ref, w_ref, kv_ref):
    kv_ref[...] = _dot(mem_ref[...].astype(BF16), w_ref[...].astype(BF16)).astype(BF16)


def _memkv(mem, w_mkv, tn=1024):
    rows = mem.shape[0]
    return pl.pallas_call(
        _memkv_kernel,
        grid=(DEPTH, 2 * D_MODEL // tn),
        in_specs=[pl.BlockSpec((rows, D_MODEL), lambda i, j: (0, 0)),
                  pl.BlockSpec((None, D_MODEL, tn), lambda i, j: (i, 0, j))],
        out_specs=pl.BlockSpec((None, rows, tn), lambda i, j: (i, 0, j)),
        out_shape=jax.ShapeDtypeStruct((DEPTH, rows, 2 * D_MODEL), BF16),
        compiler_params=_params("arbitrary", "arbitrary"),
        name="memkv",
    )(mem, w_mkv)


def _memattn_kernel(x_ref, wq_ref, k_ref, v_ref, wo_ref, g_ref, b_ref, out_ref):
    x = x_ref[...]
    q = (_dot(x.astype(BF16), wq_ref[...]) * MEM_HEAD_DIM ** -0.5).astype(BF16)
    heads = []
    for h in range(N_MEM_HEADS):
        cols = slice(h * MEM_HEAD_DIM, (h + 1) * MEM_HEAD_DIM)
        s = _dot_nt(q[:, cols], k_ref[:, cols])
        m = jnp.max(s, axis=-1, keepdims=True)
        p = jnp.exp(s - m)
        l = jnp.sum(p, axis=-1, keepdims=True)
        o = _dot(p.astype(BF16), v_ref[:, cols])
        heads.append((o * (1.0 / l)).astype(BF16))
    y = _dot(jnp.concatenate(heads, axis=1), wo_ref[...])
    out_ref[...] = _layernorm(ALPHA * x + y, g_ref[...], b_ref[...])


def _memattn(x, wq, kv, wo, g, b, batch, seq, layer, tm=512):
    n_mem = kv.shape[1] // batch
    per_batch = seq // tm
    row = lambda i: (i, 0)
    weight = _resident_weight(wq)
    kv_spec = lambda half: pl.BlockSpec((None, n_mem, D_MODEL),
                                        lambda i: (layer, i // per_batch, half))
    return pl.pallas_call(
        _memattn_kernel,
        grid=(batch * seq // tm,),
        in_specs=[pl.BlockSpec((tm, D_MODEL), row), weight, kv_spec(0), kv_spec(1), weight,
                  _layer_vec(D_MODEL, layer), _layer_vec(D_MODEL, layer)],
        out_specs=pl.BlockSpec((tm, D_MODEL), row),
        out_shape=jax.ShapeDtypeStruct((batch * seq, D_MODEL), F32),
        compiler_params=_params("arbitrary"),
        name="memattn",
    )(x, wq, kv, kv, wo, g, b)


def _mlp_kernel(x_ref, wu_ref, wd_ref, g_ref, b_ref, out_ref):
    j = pl.program_id(1)

    @pl.when(j == 0)
    def _():
        out_ref[...] = ALPHA * x_ref[...]

    h = jnp.maximum(_dot(x_ref[...].astype(BF16), wu_ref[...].astype(BF16)), 0.0)
    out_ref[...] += _dot((h * h).astype(BF16), wd_ref[...].astype(BF16))

    @pl.when(j == pl.num_programs(1) - 1)
    def _():
        out_ref[...] = _layernorm(out_ref[...], g_ref[...], b_ref[...])


def _mlp(x, wu, wd, g, b, layer, tm=1024, tf=512):
    m = x.shape[0]
    row = lambda i, j: (i, 0)
    return pl.pallas_call(
        _mlp_kernel,
        grid=(m // tm, D_FF // tf),
        in_specs=[pl.BlockSpec((tm, D_MODEL), row),
                  pl.BlockSpec((None, D_MODEL, tf), lambda i, j: (layer, 0, j)),
                  pl.BlockSpec((None, tf, D_MODEL), lambda i, j: (layer, j, 0)),
                  _layer_vec(D_MODEL, layer), _layer_vec(D_MODEL, layer)],
        out_specs=pl.BlockSpec((tm, D_MODEL), row),
        out_shape=jax.ShapeDtypeStruct((m, D_MODEL), F32),
        compiler_params=_params("arbitrary", "arbitrary"),
        name="mlp",
    )(x, wu, wd, g, b)


def _rope_lane_tables(positions, head_dim, rot_dim):
    half = rot_dim // 2
    inv_freq = ROPE_THETA ** (-jnp.arange(0, rot_dim, 2, dtype=F32) / rot_dim)
    ang = positions.astype(F32).reshape(-1, 1) * inv_freq
    cos, sin = jnp.cos(ang), jnp.sin(ang)
    rows = ang.shape[0]
    pad = jnp.zeros((rows, head_dim - rot_dim), F32)
    zero = jnp.zeros((rows, half), F32)
    reps = BLOCK // head_dim
    c = jnp.tile(jnp.concatenate([cos, cos, pad + 1.0], axis=1), (1, reps))
    s_up = jnp.tile(jnp.concatenate([zero, sin, pad], axis=1), (1, reps))
    s_dn = jnp.tile(jnp.concatenate([-sin, zero, pad], axis=1), (1, reps))
    return c, s_up, s_dn


def kernel(x, mem, positions, w_in, gn_a, gn_b, sinks, w_out, ln_mix_g, ln_mix_b, w_mq, w_mkv,
           w_mo, ln_mem_g, ln_mem_b, w_up, w_down, ln_ff_g, ln_ff_b):
    batch, seq, _ = x.shape
    tabs_a = _rope_lane_tables(positions, HEAD_DIM_A, ROT_DIM_A)
    tabs_b = _rope_lane_tables(positions, HEAD_DIM_B, ROT_DIM_B)
    (w_in_i,) = _round_layer((w_in,), 0)
    later = (w_out, w_mq, w_mo)
    gn_a, gn_b, ln_mix_g, ln_mix_b, ln_mem_g, ln_mem_b, ln_ff_g, ln_ff_b = (
        v.reshape(DEPTH, 1, -1)
        for v in (gn_a, gn_b, ln_mix_g, ln_mix_b, ln_mem_g, ln_mem_b, ln_ff_g, ln_ff_b))
    kv = _memkv(mem.reshape(-1, D_MODEL), w_mkv)
    h = x.reshape(batch * seq, D_MODEL)
    for i in range(DEPTH):
        more = i + 1 < DEPTH
        jobs = [(w_in, i + 1)] * more + [(w, 0) for w in later] * (i == 0)
        (qa, ka, va, qb, kvb), rounded = _inproj(
            h, w_in_i, tabs_a, tabs_b, [w for w, _ in jobs], [l for _, l in jobs])
        next_in = rounded[:1] if more else ()
        if i == 0:
            w_out_i, w_mq_i, w_mo_i = rounded[-3:]
        ya = _attn_a(qa, ka, va, batch, seq)
        yb = _attn_b(sinks[i], qb, kvb, batch, seq)
        h, next_rest = _outproj(ya, yb, h, w_out_i, gn_a, gn_b, ln_mix_g, ln_mix_b, i,
                                later if more else ())
        h = _memattn(h, w_mq_i, kv, w_mo_i, ln_mem_g, ln_mem_b, batch, seq, i)
        h = _mlp(h, w_up, w_down, ln_ff_g, ln_ff_b, i)
        if more:
            (w_in_i,), (w_out_i, w_mq_i, w_mo_i) = next_in, next_rest
    return h.reshape(batch, seq, D_MODEL)
```

```python
import functools
import math

import jax
import jax.numpy as jnp
from jax import lax
from jax.experimental import pallas as pl
from jax.experimental.pallas import tpu as pltpu

D_MODEL = 2048
DEPTH = 4
N_MEM_HEADS = 4
MEM_HEAD_DIM = D_MODEL // N_MEM_HEADS
HEAD_DIM_A = 128
N_HEADS_A = 8
WIDTH_A = N_HEADS_A * HEAD_DIM_A
HEAD_DIM_B = 64
N_Q_HEADS_B = 16
N_KV_HEADS_B = 2
WIDTH_B = N_Q_HEADS_B * HEAD_DIM_B
KV_WIDTH_B = N_KV_HEADS_B * HEAD_DIM_B
ROPE_THETA = 500000.0
ROT_DIM_A = HEAD_DIM_A // 4
ROT_DIM_B = HEAD_DIM_B // 4
D_FF = 4 * D_MODEL
BLOCK = 128
ALPHA = (2 * DEPTH) ** 0.25
LN_EPS = 1e-5
RMS_EPS = 1e-6
NEG_INF = -1e30
LOG2E = math.log2(math.e)

PIECE = 16
GROUP = PIECE * PIECE
GROUPS_PER_BODY = 5

BF16 = jnp.bfloat16
F32 = jnp.float32

VMEM_LIMIT_BYTES = 58 * 1024 * 1024


def _params(*semantics):
    return pltpu.CompilerParams(dimension_semantics=semantics,
                                vmem_limit_bytes=VMEM_LIMIT_BYTES)


def _single_buffered(block_shape, index_map):
    return pl.BlockSpec(block_shape, index_map, pipeline_mode=pl.Buffered(1))


def _resident_weight(w):
    return _single_buffered((None,) + w.shape[1:], lambda *_: (0, 0, 0))


def _round_job(stacks, layers, steps):
    rows = [w.shape[1] // steps for w in stacks]
    assert all(r * steps == w.shape[1] and r % PIECE == 0 for r, w in zip(rows, stacks))
    blocks = [(None, r, w.shape[2]) for r, w in zip(rows, stacks)]
    return ([pl.BlockSpec(blk, lambda i, layer=layer: (layer, i, 0))
             for blk, layer in zip(blocks, layers)],
            [pl.BlockSpec(blk, lambda i: (0, i, 0)) for blk in blocks],
            [jax.ShapeDtypeStruct((1,) + w.shape[1:], BF16) for w in stacks])


def _round_slices(src_refs, dst_refs):
    for src, dst in zip(src_refs, dst_refs):
        dst[...] = src[...].astype(BF16)


def _round_kernel(*refs):
    _round_slices(refs[:len(refs) // 2], refs[len(refs) // 2:])


def _round_layer(stacks, layer, steps=8):
    in_specs, out_specs, out_shape = _round_job(stacks, [layer] * len(stacks), steps)
    return pl.pallas_call(
        _round_kernel, grid=(steps,), in_specs=in_specs, out_specs=out_specs,
        out_shape=out_shape, compiler_params=_params("arbitrary"), name="round_weights",
    )(*stacks)


def _layernorm(z, g, b):
    mu = jnp.mean(z, axis=-1, keepdims=True)
    zc = z - mu
    var = jnp.mean(zc * zc, axis=-1, keepdims=True)
    return zc * lax.rsqrt(var + LN_EPS) * g + b


def _dot_nt(a, b):
    return lax.dot_general(a, b, (((1,), (1,)), ((), ())), preferred_element_type=F32)


def _dot(a, b):
    return jnp.dot(a, b, preferred_element_type=F32)


def _rope(t, c, s_up, s_dn, half):
    return (t * c + pltpu.roll(t, half, 1) * s_up
            + pltpu.roll(t, BLOCK - half, 1) * s_dn)


def _inproj_kernel(*refs, n_round):
    x_ref, w_ref, ca_ref, ua_ref, da_ref, cb_ref, ub_ref, db_ref = refs[:8]
    qa_ref, ka_ref, va_ref, qb_ref, kvb_ref = refs[8 + n_round:13 + n_round]
    _round_slices(refs[8:8 + n_round], refs[13 + n_round:])
    x = x_ref[...].astype(BF16)
    ca, ua, da = ca_ref[...], ua_ref[...], da_ref[...]
    cb, ub, db = cb_ref[...], ub_ref[...], db_ref[...]
    scale_a = HEAD_DIM_A ** -0.5 * LOG2E
    scale_b = HEAD_DIM_B ** -0.5 * LOG2E
    chunk = 4 * BLOCK

    def proj(col, width):
        return _dot(x, w_ref[:, col:col + width])

    for c0 in range(0, WIDTH_A, chunk):
        hq = proj(c0, chunk)
        hk = proj(WIDTH_A + c0, chunk)
        for j in range(0, chunk, BLOCK):
            sl = slice(c0 + j, c0 + j + BLOCK)
            qa_ref[:, sl] = (_rope(hq[:, j:j + BLOCK], ca, ua, da, ROT_DIM_A // 2)
                             * scale_a).astype(BF16)
            ka_ref[:, sl] = _rope(hk[:, j:j + BLOCK], ca, ua, da,
                                  ROT_DIM_A // 2).astype(BF16)
        va_ref[:, c0:c0 + chunk] = proj(2 * WIDTH_A + c0, chunk).astype(BF16)
    for c0 in range(0, WIDTH_B, chunk):
        hq = proj(3 * WIDTH_A + c0, chunk)
        for j in range(0, chunk, BLOCK):
            qb_ref[:, c0 + j:c0 + j + BLOCK] = (
                _rope(hq[:, j:j + BLOCK], cb, ub, db, ROT_DIM_B // 2) * scale_b).astype(BF16)
    hkv = proj(3 * WIDTH_A + WIDTH_B, 2 * KV_WIDTH_B)
    kvb_ref[:, :KV_WIDTH_B] = _rope(hkv[:, :KV_WIDTH_B], cb, ub, db,
                                    ROT_DIM_B // 2).astype(BF16)
    kvb_ref[:, KV_WIDTH_B:] = hkv[:, KV_WIDTH_B:].astype(BF16)


def _inproj(x, w, tabs_a, tabs_b, round_stacks=(), round_layers=(), tm=512):
    m = x.shape[0]
    row = lambda i: (i, 0)
    tab_spec = pl.BlockSpec((tm, BLOCK), row)
    out = lambda width: jax.ShapeDtypeStruct((m, width), BF16)
    round_in, round_out, round_shape = _round_job(round_stacks, round_layers, m // tm)
    outs = pl.pallas_call(
        functools.partial(_inproj_kernel, n_round=len(round_stacks)),
        grid=(m // tm,),
        in_specs=[pl.BlockSpec((tm, D_MODEL), row), _resident_weight(w)] + [tab_spec] * 6
        + round_in,
        out_specs=[pl.BlockSpec((tm, WIDTH_A), row)] * 3
        + [pl.BlockSpec((tm, WIDTH_B), row), pl.BlockSpec((tm, 2 * KV_WIDTH_B), row)]
        + round_out,
        out_shape=[out(WIDTH_A)] * 3 + [out(WIDTH_B), out(2 * KV_WIDTH_B)] + round_shape,
        compiler_params=_params("arbitrary"),
        name="inproj",
    )(x, w, *tabs_a, *tabs_b, *round_stacks)
    return outs[:5], outs[5:]


def _band_mask(max_dist):
    qi = lax.broadcasted_iota(jnp.int32, (BLOCK, 2 * BLOCK), 0) + BLOCK
    kj = lax.broadcasted_iota(jnp.int32, (BLOCK, 2 * BLOCK), 1)
    dist = qi - kj
    return (dist >= 0) & (dist <= max_dist)


def _softmax_block(q, k, v, bias):
    s = _dot_nt(q, k) + bias
    m = jnp.max(s, axis=-1, keepdims=True)
    p = jnp.exp2(s - m)
    l = jnp.sum(p, axis=-1, keepdims=True)
    return m, l, _dot(p.astype(BF16), v)


def _softmax_block_onto(q, k, v, bias, m_prev, l_prev, acc_prev):
    s = _dot_nt(q, k) + bias
    m = jnp.maximum(m_prev, jnp.max(s, axis=-1, keepdims=True))
    scale_prev = jnp.exp2(m_prev - m)
    p = jnp.exp2(s - jnp.concatenate([m] * (s.shape[1] // BLOCK), axis=1))
    l = scale_prev * l_prev + jnp.sum(p, axis=-1, keepdims=True)
    return m, l, scale_prev * acc_prev + _dot(p.astype(BF16), v)


def _piece(offset):
    return pl.ds(offset if isinstance(offset, int) else pl.multiple_of(offset, PIECE), PIECE)


def _gather(ref, offsets, cols):
    return jnp.concatenate([ref[_piece(o), cols] for o in offsets], axis=0)


def _scatter(ref, offsets, cols, val):
    for i, o in enumerate(offsets):
        ref[_piece(o), cols] = val[i * PIECE:(i + 1) * PIECE]


def _attn_a_kernel(perm_ref, q_ref, k_ref, v_ref, o_ref,
                   qt_ref, kt_ref, vt_ref, acc4_ref, m4_ref, l4_ref, acc16_ref, m16_ref, l16_ref,
                   bias_ref, *, heads):
    seq = q_ref.shape[0]
    n_groups = seq // GROUP
    perm = perm_ref[...]
    head_cols = [slice(h * HEAD_DIM_A, (h + 1) * HEAD_DIM_A) for h in range(heads)]
    all_lanes = slice(0, BLOCK)
    tile = lambda col: jnp.broadcast_to(col, (BLOCK, BLOCK))

    def i4_of(idx):
        return 64 * (idx >> 6) + 4 * (idx & 15) + ((idx >> 4) & 3)

    qi = lax.broadcasted_iota(jnp.int32, (BLOCK, 2 * BLOCK), 0)
    kj = lax.broadcasted_iota(jnp.int32, (BLOCK, 2 * BLOCK), 1)
    dist4 = BLOCK + i4_of(qi) - i4_of(kj)
    for i, band in enumerate((_band_mask(BLOCK), (dist4 >= 0) & (dist4 <= BLOCK))):
        bias_ref[i] = jnp.where(band, 0.0, NEG_INF)
        bias_ref[2 + i] = jnp.where(band & (kj >= BLOCK), 0.0, NEG_INF)

    for g in range(n_groups):
        rows = slice(g * GROUP, (g + 1) * GROUP)
        for src, dst in ((q_ref, qt_ref), (k_ref, kt_ref), (v_ref, vt_ref)):
            dst[rows, :] = _dot(perm, src[rows, :]).astype(BF16)

    def transposed_block(q_off, prev_off, band, acc_ref, m_ref, l_ref):
        if prev_off is None:
            k_off, bias = q_off + q_off, bias_ref[2 + band]
        else:
            k_off, bias = prev_off + q_off, bias_ref[band]
        for h, cols in enumerate(head_cols):
            m, l, acc = _softmax_block(_gather(qt_ref, q_off, cols), _gather(kt_ref, k_off, cols),
                                       _gather(vt_ref, k_off, cols), bias)
            _scatter(acc_ref, q_off, cols, acc)
            _scatter(m_ref.at[h], q_off, all_lanes, tile(m))
            _scatter(l_ref.at[h], q_off, all_lanes, tile(l))

    for r4 in range(4):
        for n in range(seq // 4 // BLOCK):
            q_off = [GROUP * (2 * n + g) + PIECE * (r4 + 4 * k)
                     for g in range(2) for k in range(4)]
            prev_off = [o - 2 * GROUP for o in q_off] if n else None
            transposed_block(q_off, prev_off, 1, acc4_ref, m4_ref, l4_ref)

    groups_per_block = BLOCK // PIECE
    for r in range(PIECE):
        for n in range(seq // PIECE // BLOCK):
            q_off = [GROUP * (groups_per_block * n + g) + PIECE * r
                     for g in range(groups_per_block)]
            prev_off = [o - GROUP * groups_per_block for o in q_off] if n else None
            transposed_block(q_off, prev_off, 0, acc16_ref, m16_ref, l16_ref)

    def dilation1(g, first):
        g = jnp.int32(g)
        rows = pl.ds(pl.multiple_of(g * GROUP, GROUP), GROUP)
        accs, stats = [], []
        for h, cols in enumerate(head_cols):
            m4, m16 = m4_ref[h, rows, :], m16_ref[h, rows, :]
            m = jnp.maximum(m4, m16)
            m = (m + jnp.abs(m) * 2.0 ** -7).astype(BF16)
            w4, w16 = jnp.exp2(m4 - m.astype(F32)), jnp.exp2(m16 - m.astype(F32))
            l = w4 * l4_ref[h, rows, :] + w16 * l16_ref[h, rows, :]
            l_hi = l.astype(BF16)
            accs.append((w4 * acc4_ref[rows, cols] + w16 * acc16_ref[rows, cols]).astype(BF16))
            stats += [m, l_hi, (l - l_hi.astype(F32)).astype(BF16)]
        carry = _dot(perm, jnp.concatenate(accs + stats, axis=1))
        for j in range(GROUP // BLOCK):
            local = slice(j * BLOCK, (j + 1) * BLOCK)
            r0 = pl.multiple_of(g * GROUP + j * BLOCK, BLOCK)
            for h, cols in enumerate(head_cols):
                stats = heads * HEAD_DIM_A + 3 * BLOCK * h
                m_prev = carry[local, stats:stats + BLOCK]
                l_prev = (carry[local, stats + BLOCK:stats + 2 * BLOCK]
                          + carry[local, stats + 2 * BLOCK:stats + 3 * BLOCK])
                if first and j == 0:
                    p0, bias = (r0, r0), bias_ref[2]
                else:
                    p0 = pl.multiple_of(g * GROUP + (j - 1) * BLOCK, BLOCK)
                    p0, bias = (p0, r0), bias_ref[0]
                k2 = jnp.concatenate([k_ref[pl.ds(p, BLOCK), cols] for p in p0], axis=0)
                v2 = jnp.concatenate([v_ref[pl.ds(p, BLOCK), cols] for p in p0], axis=0)
                _, l, acc = _softmax_block_onto(q_ref[pl.ds(r0, BLOCK), cols], k2, v2, bias,
                                                m_prev, l_prev, carry[local, cols])
                o_ref[pl.ds(r0, BLOCK), cols] = (acc * (1.0 / l)).astype(BF16)

    def later_groups(t, carry):
        for i in range(GROUPS_PER_BODY):
            dilation1(1 + GROUPS_PER_BODY * t + i, False)
        return carry

    dilation1(0, True)
    lax.fori_loop(0, (n_groups - 1) // GROUPS_PER_BODY, later_groups, 0)


def _group_transpose_matrix():
    idx = jnp.arange(GROUP)
    return (idx[:, None] == PIECE * (idx[None, :] % PIECE) + idx[None, :] // PIECE).astype(BF16)


def _attn_a(qa, ka, va, batch, seq, heads=2):
    assert (seq // GROUP - 1) % GROUPS_PER_BODY == 0
    width = heads * HEAD_DIM_A
    view = lambda t: t.reshape(batch, seq, WIDTH_A)
    blk = pl.BlockSpec((None, seq, width), lambda b, h: (b, 0, h))
    stat = pltpu.VMEM((heads, seq, BLOCK), F32)
    o = pl.pallas_call(
        functools.partial(_attn_a_kernel, heads=heads),
        grid=(batch, N_HEADS_A // heads),
        in_specs=[pl.BlockSpec((GROUP, GROUP), lambda b, h: (0, 0)), blk, blk, blk],
        out_specs=blk,
        out_shape=jax.ShapeDtypeStruct((batch, seq, WIDTH_A), BF16),
        scratch_shapes=[pltpu.VMEM((seq, width), BF16)] * 3
        + [pltpu.VMEM((seq, width), F32), stat, stat] * 2
        + [pltpu.VMEM((4, BLOCK, 2 * BLOCK), F32)],
        compiler_params=_params("arbitrary", "arbitrary"),
        name="attn_a",
    )(_group_transpose_matrix(), view(qa), view(ka), view(va))
    return o.reshape(batch * seq, WIDTH_A)


def _attn_b_kernel(sink_ref, q_ref, kvp_ref, kv_ref, o_ref, *, rows):
    stack = lambda t: jnp.concatenate([t, t], axis=0)
    qi = lax.broadcasted_iota(jnp.int32, (BLOCK, BLOCK), 0)
    kj = lax.broadcasted_iota(jnp.int32, (BLOCK, BLOCK), 1)
    from_prev = stack(kj > qi)
    no_prev = jnp.where(pl.program_id(1) == 0, NEG_INF, 0.0)
    low = kj < HEAD_DIM_B
    low2 = stack(low)
    row2 = lax.broadcasted_iota(jnp.int32, (2 * BLOCK, 1), 0)
    heads_per_group = N_Q_HEADS_B // N_KV_HEADS_B
    for j in range(rows // BLOCK):
        r0 = j * BLOCK
        if j == 0:
            kv2 = jnp.concatenate([kvp_ref[...], kv_ref[:BLOCK, :]], axis=0)
        else:
            kv2 = kv_ref[r0 - BLOCK:r0 + BLOCK, :]
        kv2 = kv2.astype(F32)
        k2, v2 = kv2[:, :KV_WIDTH_B], kv2[:, KV_WIDTH_B:]
        k2r, v2r = pltpu.roll(k2, HEAD_DIM_B, 1), pltpu.roll(v2, HEAD_DIM_B, 1)
        kdup = [jnp.where(low2, k2, k2r).astype(BF16), jnp.where(low2, k2r, k2).astype(BF16)]
        vdup = [jnp.where(low2, v2, v2r).astype(BF16), jnp.where(low2, v2r, v2).astype(BF16)]
        for t in range(WIDTH_B // BLOCK):
            g = (2 * t) // heads_per_group
            cols = slice(t * BLOCK, (t + 1) * BLOCK)
            q = q_ref[r0:r0 + BLOCK, cols]
            zero = jnp.zeros_like(q)
            q2 = jnp.concatenate([jnp.where(low, q, zero), jnp.where(low, zero, q)], axis=0)
            s2 = _dot_nt(q2, kdup[g])
            s_prev = s2[:, :BLOCK] + no_prev if j == 0 else s2[:, :BLOCK]
            s = jnp.where(from_prev, s_prev, s2[:, BLOCK:])
            m = jnp.max(s, axis=-1, keepdims=True)
            p = jnp.exp2(s - m)
            l = jnp.sum(p, axis=-1, keepdims=True)
            p2 = jnp.concatenate([jnp.where(from_prev, p, 0.0), jnp.where(from_prev, 0.0, p)],
                                 axis=1)
            acc = _dot(p2.astype(BF16), vdup[g])
            sink = jnp.where(row2 < BLOCK, sink_ref[2 * t], sink_ref[2 * t + 1]) * LOG2E
            y = acc * (1.0 / (l + jnp.exp2(sink - m)))
            o_ref[r0:r0 + BLOCK, cols] = jnp.where(low, y[:BLOCK], y[BLOCK:]).astype(BF16)


def _attn_b(sinks, qb, kvb, batch, seq, rows=1024):
    nblk = seq // rows
    prev_per_blk = rows // BLOCK
    view = lambda t: t.reshape(batch, seq, t.shape[-1])
    cur = lambda b, n: (b, n, 0)
    prev = lambda b, n: (b, jnp.maximum(n * prev_per_blk - 1, 0), 0)
    o = pl.pallas_call(
        functools.partial(_attn_b_kernel, rows=rows),
        grid=(batch, nblk),
        in_specs=[pl.BlockSpec(memory_space=pltpu.SMEM),
                  pl.BlockSpec((None, rows, WIDTH_B), cur),
                  pl.BlockSpec((None, BLOCK, 2 * KV_WIDTH_B), prev),
                  pl.BlockSpec((None, rows, 2 * KV_WIDTH_B), cur)],
        out_specs=pl.BlockSpec((None, rows, WIDTH_B), cur),
        out_shape=jax.ShapeDtypeStruct((batch, seq, WIDTH_B), BF16),
        compiler_params=_params("arbitrary", "arbitrary"),
        name="attn_b",
    )(sinks, view(qb), view(kvb), view(kvb))
    return o.reshape(batch * seq, WIDTH_B)


def _rms_gain(y, gain):
    return y * lax.rsqrt(jnp.mean(y * y, axis=-1, keepdims=True) + RMS_EPS) * gain


def _outproj_kernel(*refs, n_round):
    ya_ref, yb_ref, x_ref, w_ref, gna_ref, gnb_ref, g_ref, b_ref = refs[:8]
    out_ref = refs[8 + n_round]
    _round_slices(refs[8:8 + n_round], refs[9 + n_round:])
    ya = _rms_gain(ya_ref[...].astype(F32), gna_ref[...]).astype(BF16)
    yb = _rms_gain(yb_ref[...].astype(F32), gnb_ref[...]).astype(BF16)
    y = _dot(jnp.concatenate([ya, yb], axis=1), w_ref[...])
    out_ref[...] = _layernorm(ALPHA * x_ref[...] + y, g_ref[...], b_ref[...])


def _layer_vec(width, layer):
    return pl.BlockSpec((None, 1, width), lambda *_: (layer, 0, 0))


def _outproj(ya, yb, x, w, gna, gnb, g, b, layer, round_stacks=(), tm=512):
    m = x.shape[0]
    row = lambda i: (i, 0)
    round_in, round_out, round_shape = _round_job(
        round_stacks, [layer + 1] * len(round_stacks), m // tm)
    outs = pl.pallas_call(
        functools.partial(_outproj_kernel, n_round=len(round_stacks)),
        grid=(m // tm,),
        in_specs=[pl.BlockSpec((tm, WIDTH_A), row), pl.BlockSpec((tm, WIDTH_B), row),
                  pl.BlockSpec((tm, D_MODEL), row),
                  _resident_weight(w),
                  _layer_vec(WIDTH_A, layer), _layer_vec(WIDTH_B, layer),
                  _layer_vec(D_MODEL, layer), _layer_vec(D_MODEL, layer)] + round_in,
        out_specs=[pl.BlockSpec((tm, D_MODEL), row)] + round_out,
        out_shape=[jax.ShapeDtypeStruct((m, D_MODEL), F32)] + round_shape,
        compiler_params=_params("arbitrary"),
        name="outproj",
    )(ya, yb, x, w, gna, gnb, g, b, *round_stacks)
    return outs[0], outs[1:]


def _memkv_kernel(mem_ref, w_ref, kv_ref):
    kv_ref[...] = _dot(mem_ref[...].astype(BF16), w_ref[...].astype(BF16)).astype(BF16)


def _memkv(mem, w_mkv, tn=2048):
    rows = mem.shape[0]
    return pl.pallas_call(
        _memkv_kernel,
        grid=(DEPTH, 2 * D_MODEL // tn),
        in_specs=[pl.BlockSpec((rows, D_MODEL), lambda i, j: (0, 0)),
                  pl.BlockSpec((None, D_MODEL, tn), lambda i, j: (i, 0, j))],
        out_specs=pl.BlockSpec((None, rows, tn), lambda i, j: (i, 0, j)),
        out_shape=jax.ShapeDtypeStruct((DEPTH, rows, 2 * D_MODEL), BF16),
        compiler_params=_params("arbitrary", "arbitrary"),
        name="memkv",
    )(mem, w_mkv)


def _memattn_kernel(x_ref, wq_ref, k_ref, v_ref, wo_ref, g_ref, b_ref, out_ref):
    x = x_ref[...]
    q = (_dot(x.astype(BF16), wq_ref[...]) * MEM_HEAD_DIM ** -0.5).astype(BF16)
    heads = []
    for h in range(N_MEM_HEADS):
        cols = slice(h * MEM_HEAD_DIM, (h + 1) * MEM_HEAD_DIM)
        s = _dot_nt(q[:, cols], k_ref[:, cols])
        m = jnp.max(s, axis=-1, keepdims=True)
        p = jnp.exp(s - m)
        l = jnp.sum(p, axis=-1, keepdims=True)
        o = _dot(p.astype(BF16), v_ref[:, cols])
        heads.append((o * (1.0 / l)).astype(BF16))
    y = _dot(jnp.concatenate(heads, axis=1), wo_ref[...])
    out_ref[...] = _layernorm(ALPHA * x + y, g_ref[...], b_ref[...])


def _memattn(x, wq, kv, wo, g, b, batch, seq, layer, tm=512):
    n_mem = kv.shape[1] // batch
    per_batch = seq // tm
    row = lambda i: (i, 0)
    weight = _resident_weight(wq)
    kv_spec = lambda half: pl.BlockSpec((None, n_mem, D_MODEL),
                                        lambda i: (layer, i // per_batch, half))
    return pl.pallas_call(
        _memattn_kernel,
        grid=(batch * seq // tm,),
        in_specs=[pl.BlockSpec((tm, D_MODEL), row), weight, kv_spec(0), kv_spec(1), weight,
                  _layer_vec(D_MODEL, layer), _layer_vec(D_MODEL, layer)],
        out_specs=pl.BlockSpec((tm, D_MODEL), row),
        out_shape=jax.ShapeDtypeStruct((batch * seq, D_MODEL), F32),
        compiler_params=_params("arbitrary"),
        name="memattn",
    )(x, wq, kv, kv, wo, g, b)


def _mlp_kernel(x_ref, wu_ref, wd_ref, g_ref, b_ref, out_ref):
    j = pl.program_id(1)

    @pl.when(j == 0)
    def _():
        out_ref[...] = ALPHA * x_ref[...]

    h = jnp.maximum(_dot(x_ref[...].astype(BF16), wu_ref[...].astype(BF16)), 0.0)
    out_ref[...] += _dot((h * h).astype(BF16), wd_ref[...].astype(BF16))

    @pl.when(j == pl.num_programs(1) - 1)
    def _():
        out_ref[...] = _layernorm(out_ref[...], g_ref[...], b_ref[...])


def _mlp(x, wu, wd, g, b, layer, tm=1024, tf=512):
    m = x.shape[0]
    row = lambda i, j: (i, 0)
    return pl.pallas_call(
        _mlp_kernel,
        grid=(m // tm, D_FF // tf),
        in_specs=[pl.BlockSpec((tm, D_MODEL), row),
                  pl.BlockSpec((None, D_MODEL, tf), lambda i, j: (layer, 0, j)),
                  pl.BlockSpec((None, tf, D_MODEL), lambda i, j: (layer, j, 0)),
                  _layer_vec(D_MODEL, layer), _layer_vec(D_MODEL, layer)],
        out_specs=pl.BlockSpec((tm, D_MODEL), row),
        out_shape=jax.ShapeDtypeStruct((m, D_MODEL), F32),
        compiler_params=_params("arbitrary", "arbitrary"),
        name="mlp",
    )(x, wu, wd, g, b)


def _rope_lane_tables(positions, head_dim, rot_dim):
    half = rot_dim // 2
    inv_freq = ROPE_THETA ** (-jnp.arange(0, rot_dim, 2, dtype=F32) / rot_dim)
    ang = positions.astype(F32).reshape(-1, 1) * inv_freq
    cos, sin = jnp.cos(ang), jnp.sin(ang)
    rows = ang.shape[0]
    pad = jnp.zeros((rows, head_dim - rot_dim), F32)
    zero = jnp.zeros((rows, half), F32)
    reps = BLOCK // head_dim
    c = jnp.tile(jnp.concatenate([cos, cos, pad + 1.0], axis=1), (1, reps))
    s_up = jnp.tile(jnp.concatenate([zero, sin, pad], axis=1), (1, reps))
    s_dn = jnp.tile(jnp.concatenate([-sin, zero, pad], axis=1), (1, reps))
    return c, s_up, s_dn


def kernel(x, mem, positions, w_in, gn_a, gn_b, sinks, w_out, ln_mix_g, ln_mix_b, w_mq, w_mkv,
           w_mo, ln_mem_g, ln_mem_b, w_up, w_down, ln_ff_g, ln_ff_b):
    batch, seq, _ = x.shape
    tabs_a = _rope_lane_tables(positions, HEAD_DIM_A, ROT_DIM_A)
    tabs_b = _rope_lane_tables(positions, HEAD_DIM_B, ROT_DIM_B)
    (w_in_i,) = _round_layer((w_in,), 0)
    later = (w_out, w_mq, w_mo)
    gn_a, gn_b, ln_mix_g, ln_mix_b, ln_mem_g, ln_mem_b, ln_ff_g, ln_ff_b = (
        v.reshape(DEPTH, 1, -1)
        for v in (gn_a, gn_b, ln_mix_g, ln_mix_b, ln_mem_g, ln_mem_b, ln_ff_g, ln_ff_b))
    kv = _memkv(mem.reshape(-1, D_MODEL), w_mkv)
    h = x.reshape(batch * seq, D_MODEL)
    for i in range(DEPTH):
        more = i + 1 < DEPTH
        jobs = [(w_in, i + 1)] * more + [(w, 0) for w in later] * (i == 0)
        (qa, ka, va, qb, kvb), rounded = _inproj(
            h, w_in_i, tabs_a, tabs_b, [w for w, _ in jobs], [l for _, l in jobs])
        next_in = rounded[:1] if more else ()
        if i == 0:
            w_out_i, w_mq_i, w_mo_i = rounded[-3:]
        ya = _attn_a(qa, ka, va, batch, seq)
        yb = _attn_b(sinks[i], qb, kvb, batch, seq)
        h, next_rest = _outproj(ya, yb, h, w_out_i, gn_a, gn_b, ln_mix_g, ln_mix_b, i,
                                later if more else ())
        h = _memattn(h, w_mq_i, kv, w_mo_i, ln_mem_g, ln_mem_b, batch, seq, i)
        h = _mlp(h, w_up, w_down, ln_ff_g, ln_ff_b, i)
        if more:
            (w_in_i,), (w_out_i, w_mq_i, w_mo_i) = next_in, next_rest
    return h.reshape(batch, seq, D_MODEL)
```

```python
import functools
import math

import jax
import jax.numpy as jnp
from jax import lax
from jax.experimental import pallas as pl
from jax.experimental.pallas import tpu as pltpu

D_MODEL = 2048
DEPTH = 4
N_MEM_HEADS = 4
MEM_HEAD_DIM = D_MODEL // N_MEM_HEADS
HEAD_DIM_A = 128
N_HEADS_A = 8
WIDTH_A = N_HEADS_A * HEAD_DIM_A
HEAD_DIM_B = 64
N_Q_HEADS_B = 16
N_KV_HEADS_B = 2
WIDTH_B = N_Q_HEADS_B * HEAD_DIM_B
KV_WIDTH_B = N_KV_HEADS_B * HEAD_DIM_B
ROPE_THETA = 500000.0
ROT_DIM_A = HEAD_DIM_A // 4
ROT_DIM_B = HEAD_DIM_B // 4
D_FF = 4 * D_MODEL
BLOCK = 128
ALPHA = (2 * DEPTH) ** 0.25
LN_EPS = 1e-5
RMS_EPS = 1e-6
NEG_INF = -1e30
LOG2E = math.log2(math.e)

PIECE = 16
GROUP = PIECE * PIECE
GROUPS_PER_BODY = 5

BF16 = jnp.bfloat16
F32 = jnp.float32

VMEM_LIMIT_BYTES = 58 * 1024 * 1024


def _params(*semantics):
    return pltpu.CompilerParams(dimension_semantics=semantics,
                                vmem_limit_bytes=VMEM_LIMIT_BYTES)


def _single_buffered(block_shape, index_map):
    return pl.BlockSpec(block_shape, index_map, pipeline_mode=pl.Buffered(1))


def _resident_weight(w):
    return _single_buffered((None,) + w.shape[1:], lambda *_: (0, 0, 0))


def _round_job(stacks, layers, steps):
    rows = [w.shape[1] // steps for w in stacks]
    assert all(r * steps == w.shape[1] and r % PIECE == 0 for r, w in zip(rows, stacks))
    blocks = [(None, r, w.shape[2]) for r, w in zip(rows, stacks)]
    return ([pl.BlockSpec(blk, lambda i, layer=layer: (layer, i, 0))
             for blk, layer in zip(blocks, layers)],
            [pl.BlockSpec(blk, lambda i: (0, i, 0)) for blk in blocks],
            [jax.ShapeDtypeStruct((1,) + w.shape[1:], BF16) for w in stacks])


def _round_slices(src_refs, dst_refs):
    for src, dst in zip(src_refs, dst_refs):
        dst[...] = src[...].astype(BF16)


def _round_kernel(*refs):
    _round_slices(refs[:len(refs) // 2], refs[len(refs) // 2:])


def _round_layer(stacks, layer, steps=8):
    in_specs, out_specs, out_shape = _round_job(stacks, [layer] * len(stacks), steps)
    return pl.pallas_call(
        _round_kernel, grid=(steps,), in_specs=in_specs, out_specs=out_specs,
        out_shape=out_shape, compiler_params=_params("arbitrary"), name="round_weights",
    )(*stacks)


def _layernorm(z, g, b):
    mu = jnp.mean(z, axis=-1, keepdims=True)
    zc = z - mu
    var = jnp.mean(zc * zc, axis=-1, keepdims=True)
    return zc * lax.rsqrt(var + LN_EPS) * g + b


def _dot_nt(a, b):
    return lax.dot_general(a, b, (((1,), (1,)), ((), ())), preferred_element_type=F32)


def _dot(a, b):
    return jnp.dot(a, b, preferred_element_type=F32)


def _rope(t, c, s_up, s_dn, half):
    return (t * c + pltpu.roll(t, half, 1) * s_up
            + pltpu.roll(t, BLOCK - half, 1) * s_dn)


def _inproj_kernel(*refs, n_round):
    x_ref, w_ref, ca_ref, ua_ref, da_ref, cb_ref, ub_ref, db_ref = refs[:8]
    qa_ref, ka_ref, va_ref, qb_ref, kvb_ref = refs[8 + n_round:13 + n_round]
    _round_slices(refs[8:8 + n_round], refs[13 + n_round:])
    x = x_ref[...].astype(BF16)
    ca, ua, da = ca_ref[...], ua_ref[...], da_ref[...]
    cb, ub, db = cb_ref[...], ub_ref[...], db_ref[...]
    scale_a = HEAD_DIM_A ** -0.5 * LOG2E
    scale_b = HEAD_DIM_B ** -0.5 * LOG2E
    chunk = 4 * BLOCK

    def proj(col, width):
        return _dot(x, w_ref[:, col:col + width])

    for c0 in range(0, WIDTH_A, chunk):
        hq = proj(c0, chunk)
        hk = proj(WIDTH_A + c0, chunk)
        for j in range(0, chunk, BLOCK):
            sl = slice(c0 + j, c0 + j + BLOCK)
            qa_ref[:, sl] = (_rope(hq[:, j:j + BLOCK], ca, ua, da, ROT_DIM_A // 2)
                             * scale_a).astype(BF16)
            ka_ref[:, sl] = _rope(hk[:, j:j + BLOCK], ca, ua, da,
                                  ROT_DIM_A // 2).astype(BF16)
        va_ref[:, c0:c0 + chunk] = proj(2 * WIDTH_A + c0, chunk).astype(BF16)
    for c0 in range(0, WIDTH_B, chunk):
        hq = proj(3 * WIDTH_A + c0, chunk)
        for j in range(0, chunk, BLOCK):
            qb_ref[:, c0 + j:c0 + j + BLOCK] = (
                _rope(hq[:, j:j + BLOCK], cb, ub, db, ROT_DIM_B // 2) * scale_b).astype(BF16)
    hkv = proj(3 * WIDTH_A + WIDTH_B, 2 * KV_WIDTH_B)
    kvb_ref[:, :KV_WIDTH_B] = _rope(hkv[:, :KV_WIDTH_B], cb, ub, db,
                                    ROT_DIM_B // 2).astype(BF16)
    kvb_ref[:, KV_WIDTH_B:] = hkv[:, KV_WIDTH_B:].astype(BF16)


def _inproj(x, w, tabs_a, tabs_b, round_stacks=(), round_layers=(), tm=512):
    m = x.shape[0]
    row = lambda i: (i, 0)
    tab_spec = pl.BlockSpec((tm, BLOCK), row)
    out = lambda width: jax.ShapeDtypeStruct((m, width), BF16)
    round_in, round_out, round_shape = _round_job(round_stacks, round_layers, m // tm)
    outs = pl.pallas_call(
        functools.partial(_inproj_kernel, n_round=len(round_stacks)),
        grid=(m // tm,),
        in_specs=[pl.BlockSpec((tm, D_MODEL), row), _resident_weight(w)] + [tab_spec] * 6
        + round_in,
        out_specs=[pl.BlockSpec((tm, WIDTH_A), row)] * 3
        + [pl.BlockSpec((tm, WIDTH_B), row), pl.BlockSpec((tm, 2 * KV_WIDTH_B), row)]
        + round_out,
        out_shape=[out(WIDTH_A)] * 3 + [out(WIDTH_B), out(2 * KV_WIDTH_B)] + round_shape,
        compiler_params=_params("arbitrary"),
        name="inproj",
    )(x, w, *tabs_a, *tabs_b, *round_stacks)
    return outs[:5], outs[5:]


def _band_mask(max_dist):
    qi = lax.broadcasted_iota(jnp.int32, (BLOCK, 2 * BLOCK), 0) + BLOCK
    kj = lax.broadcasted_iota(jnp.int32, (BLOCK, 2 * BLOCK), 1)
    dist = qi - kj
    return (dist >= 0) & (dist <= max_dist)


def _weighted_values(p, v):
    pv = _dot(p.astype(BF16), jnp.concatenate([v, jnp.ones_like(v)], axis=1))
    return pv[:, BLOCK:], pv[:, :BLOCK]


def _softmax_block(q, k, v, bias):
    s = _dot_nt(q, k) + bias
    m = jnp.max(s, axis=-1, keepdims=True)
    l, acc = _weighted_values(jnp.exp2(s - m), v)
    return m, l, acc


def _softmax_block_onto(q, k, v, bias, m_prev, l_prev, acc_prev):
    s = _dot_nt(q, k) + bias
    m = jnp.maximum(m_prev, jnp.max(s, axis=-1, keepdims=True))
    scale_prev = jnp.exp2(m_prev - m)
    l, acc = _weighted_values(
        jnp.exp2(s - jnp.concatenate([m] * (s.shape[1] // BLOCK), axis=1)), v)
    return m, scale_prev * l_prev + l, scale_prev * acc_prev + acc


def _piece(offset):
    return pl.ds(offset if isinstance(offset, int) else pl.multiple_of(offset, PIECE), PIECE)


def _gather(ref, offsets, cols):
    return jnp.concatenate([ref[_piece(o), cols] for o in offsets], axis=0)


def _scatter(ref, offsets, cols, val):
    for i, o in enumerate(offsets):
        ref[_piece(o), cols] = val[i * PIECE:(i + 1) * PIECE]


def _attn_a_kernel(perm_ref, q_ref, k_ref, v_ref, o_ref,
                   qt_ref, kt_ref, vt_ref, acc4_ref, m4_ref, l4_ref, acc16_ref, m16_ref, l16_ref,
                   bias_ref, *, heads):
    seq = q_ref.shape[0]
    n_groups = seq // GROUP
    perm = perm_ref[...]
    head_cols = [slice(h * HEAD_DIM_A, (h + 1) * HEAD_DIM_A) for h in range(heads)]
    all_lanes = slice(0, BLOCK)
    tile = lambda col: jnp.broadcast_to(col, (BLOCK, BLOCK))

    def i4_of(idx):
        return 64 * (idx >> 6) + 4 * (idx & 15) + ((idx >> 4) & 3)

    qi = lax.broadcasted_iota(jnp.int32, (BLOCK, 2 * BLOCK), 0)
    kj = lax.broadcasted_iota(jnp.int32, (BLOCK, 2 * BLOCK), 1)
    dist4 = BLOCK + i4_of(qi) - i4_of(kj)
    for i, band in enumerate((_band_mask(BLOCK), (dist4 >= 0) & (dist4 <= BLOCK))):
        bias_ref[i] = jnp.where(band, 0.0, NEG_INF)
        bias_ref[2 + i] = jnp.where(band & (kj >= BLOCK), 0.0, NEG_INF)

    for g in range(n_groups):
        rows = slice(g * GROUP, (g + 1) * GROUP)
        for src, dst in ((q_ref, qt_ref), (k_ref, kt_ref), (v_ref, vt_ref)):
            dst[rows, :] = _dot(perm, src[rows, :]).astype(BF16)

    def transposed_block(q_off, prev_off, band, acc_ref, m_ref, l_ref):
        if prev_off is None:
            k_off, bias = q_off + q_off, bias_ref[2 + band]
        else:
            k_off, bias = prev_off + q_off, bias_ref[band]
        for h, cols in enumerate(head_cols):
            m, l, acc = _softmax_block(_gather(qt_ref, q_off, cols), _gather(kt_ref, k_off, cols),
                                       _gather(vt_ref, k_off, cols), bias)
            _scatter(acc_ref, q_off, cols, acc)
            _scatter(m_ref.at[h], q_off, all_lanes, tile(m))
            _scatter(l_ref.at[h], q_off, all_lanes, tile(l))

    for r4 in range(4):
        for n in range(seq // 4 // BLOCK):
            q_off = [GROUP * (2 * n + g) + PIECE * (r4 + 4 * k)
                     for g in range(2) for k in range(4)]
            prev_off = [o - 2 * GROUP for o in q_off] if n else None
            transposed_block(q_off, prev_off, 1, acc4_ref, m4_ref, l4_ref)

    groups_per_block = BLOCK // PIECE
    for r in range(PIECE):
        for n in range(seq // PIECE // BLOCK):
            q_off = [GROUP * (groups_per_block * n + g) + PIECE * r
                     for g in range(groups_per_block)]
            prev_off = [o - GROUP * groups_per_block for o in q_off] if n else None
            transposed_block(q_off, prev_off, 0, acc16_ref, m16_ref, l16_ref)

    def dilation1(g, first):
        g = jnp.int32(g)
        rows = pl.ds(pl.multiple_of(g * GROUP, GROUP), GROUP)
        accs, stats = [], []
        for h, cols in enumerate(head_cols):
            m4, m16 = m4_ref[h, rows, :], m16_ref[h, rows, :]
            m = jnp.maximum(m4, m16)
            m = (m + jnp.abs(m) * 2.0 ** -7).astype(BF16)
            w4, w16 = jnp.exp2(m4 - m.astype(F32)), jnp.exp2(m16 - m.astype(F32))
            l = w4 * l4_ref[h, rows, :] + w16 * l16_ref[h, rows, :]
            l_hi = l.astype(BF16)
            accs.append((w4 * acc4_ref[rows, cols] + w16 * acc16_ref[rows, cols]).astype(BF16))
            stats += [m, l_hi, (l - l_hi.astype(F32)).astype(BF16)]
        carry = _dot(perm, jnp.concatenate(accs + stats, axis=1))
        for j in range(GROUP // BLOCK):
            local = slice(j * BLOCK, (j + 1) * BLOCK)
            r0 = pl.multiple_of(g * GROUP + j * BLOCK, BLOCK)
            for h, cols in enumerate(head_cols):
                stats = heads * HEAD_DIM_A + 3 * BLOCK * h
                m_prev = carry[local, stats:stats + BLOCK]
                l_prev = (carry[local, stats + BLOCK:stats + 2 * BLOCK]
                          + carry[local, stats + 2 * BLOCK:stats + 3 * BLOCK])
                if first and j == 0:
                    p0, bias = (r0, r0), bias_ref[2]
                else:
                    p0 = pl.multiple_of(g * GROUP + (j - 1) * BLOCK, BLOCK)
                    p0, bias = (p0, r0), bias_ref[0]
                k2 = jnp.concatenate([k_ref[pl.ds(p, BLOCK), cols] for p in p0], axis=0)
                v2 = jnp.concatenate([v_ref[pl.ds(p, BLOCK), cols] for p in p0], axis=0)
                _, l, acc = _softmax_block_onto(q_ref[pl.ds(r0, BLOCK), cols], k2, v2, bias,
                                                m_prev, l_prev, carry[local, cols])
                o_ref[pl.ds(r0, BLOCK), cols] = (acc * (1.0 / l)).astype(BF16)

    def later_groups(t, carry):
        for i in range(GROUPS_PER_BODY):
            dilation1(1 + GROUPS_PER_BODY * t + i, False)
        return carry

    dilation1(0, True)
    lax.fori_loop(0, (n_groups - 1) // GROUPS_PER_BODY, later_groups, 0)


def _group_transpose_matrix():
    idx = jnp.arange(GROUP)
    return (idx[:, None] == PIECE * (idx[None, :] % PIECE) + idx[None, :] // PIECE).astype(BF16)


def _attn_a(qa, ka, va, batch, seq, heads=2):
    assert (seq // GROUP - 1) % GROUPS_PER_BODY == 0
    width = heads * HEAD_DIM_A
    view = lambda t: t.reshape(batch, seq, WIDTH_A)
    blk = pl.BlockSpec((None, seq, width), lambda b, h: (b, 0, h))
    stat = pltpu.VMEM((heads, seq, BLOCK), F32)
    o = pl.pallas_call(
        functools.partial(_attn_a_kernel, heads=heads),
        grid=(batch, N_HEADS_A // heads),
        in_specs=[pl.BlockSpec((GROUP, GROUP), lambda b, h: (0, 0)), blk, blk, blk],
        out_specs=blk,
        out_shape=jax.ShapeDtypeStruct((batch, seq, WIDTH_A), BF16),
        scratch_shapes=[pltpu.VMEM((seq, width), BF16)] * 3
        + [pltpu.VMEM((seq, width), F32), stat, stat] * 2
        + [pltpu.VMEM((4, BLOCK, 2 * BLOCK), F32)],
        compiler_params=_params("arbitrary", "arbitrary"),
        name="attn_a",
    )(_group_transpose_matrix(), view(qa), view(ka), view(va))
    return o.reshape(batch * seq, WIDTH_A)


def _attn_b_kernel(sink_ref, q_ref, kvp_ref, kv_ref, o_ref, *, rows):
    stack = lambda t: jnp.concatenate([t, t], axis=0)
    qi = lax.broadcasted_iota(jnp.int32, (BLOCK, BLOCK), 0)
    kj = lax.broadcasted_iota(jnp.int32, (BLOCK, BLOCK), 1)
    from_prev = stack(kj > qi)
    no_prev = jnp.where(pl.program_id(1) == 0, NEG_INF, 0.0)
    low = kj < HEAD_DIM_B
    low2 = stack(low)
    row2 = lax.broadcasted_iota(jnp.int32, (2 * BLOCK, 1), 0)
    heads_per_group = N_Q_HEADS_B // N_KV_HEADS_B
    for j in range(rows // BLOCK):
        r0 = j * BLOCK
        if j == 0:
            kv2 = jnp.concatenate([kvp_ref[...], kv_ref[:BLOCK, :]], axis=0)
        else:
            kv2 = kv_ref[r0 - BLOCK:r0 + BLOCK, :]
        kv2 = kv2.astype(F32)
        k2, v2 = kv2[:, :KV_WIDTH_B], kv2[:, KV_WIDTH_B:]
        k2r, v2r = pltpu.roll(k2, HEAD_DIM_B, 1), pltpu.roll(v2, HEAD_DIM_B, 1)
        kdup = [jnp.where(low2, k2, k2r).astype(BF16), jnp.where(low2, k2r, k2).astype(BF16)]
        vdup = [jnp.where(low2, v2, v2r).astype(BF16), jnp.where(low2, v2r, v2).astype(BF16)]
        for t in range(WIDTH_B // BLOCK):
            g = (2 * t) // heads_per_group
            cols = slice(t * BLOCK, (t + 1) * BLOCK)
            q = q_ref[r0:r0 + BLOCK, cols]
            zero = jnp.zeros_like(q)
            q2 = jnp.concatenate([jnp.where(low, q, zero), jnp.where(low, zero, q)], axis=0)
            s2 = _dot_nt(q2, kdup[g])
            s_prev = s2[:, :BLOCK] + no_prev if j == 0 else s2[:, :BLOCK]
            s = jnp.where(from_prev, s_prev, s2[:, BLOCK:])
            m = jnp.max(s, axis=-1, keepdims=True)
            p = jnp.exp2(s - m)
            l = jnp.sum(p, axis=-1, keepdims=True)
            p2 = jnp.concatenate([jnp.where(from_prev, p, 0.0), jnp.where(from_prev, 0.0, p)],
                                 axis=1)
            acc = _dot(p2.astype(BF16), vdup[g])
            sink = jnp.where(row2 < BLOCK, sink_ref[2 * t], sink_ref[2 * t + 1]) * LOG2E
            y = acc * (1.0 / (l + jnp.exp2(sink - m)))
            o_ref[r0:r0 + BLOCK, cols] = jnp.where(low, y[:BLOCK], y[BLOCK:]).astype(BF16)


def _attn_b(sinks, qb, kvb, batch, seq, rows=1024):
    nblk = seq // rows
    prev_per_blk = rows // BLOCK
    view = lambda t: t.reshape(batch, seq, t.shape[-1])
    cur = lambda b, n: (b, n, 0)
    prev = lambda b, n: (b, jnp.maximum(n * prev_per_blk - 1, 0), 0)
    o = pl.pallas_call(
        functools.partial(_attn_b_kernel, rows=rows),
        grid=(batch, nblk),
        in_specs=[pl.BlockSpec(memory_space=pltpu.SMEM),
                  pl.BlockSpec((None, rows, WIDTH_B), cur),
                  pl.BlockSpec((None, BLOCK, 2 * KV_WIDTH_B), prev),
                  pl.BlockSpec((None, rows, 2 * KV_WIDTH_B), cur)],
        out_specs=pl.BlockSpec((None, rows, WIDTH_B), cur),
        out_shape=jax.ShapeDtypeStruct((batch, seq, WIDTH_B), BF16),
        compiler_params=_params("arbitrary", "arbitrary"),
        name="attn_b",
    )(sinks, view(qb), view(kvb), view(kvb))
    return o.reshape(batch * seq, WIDTH_B)


def _rms_gain(y, gain):
    return y * lax.rsqrt(jnp.mean(y * y, axis=-1, keepdims=True) + RMS_EPS) * gain


def _outproj_kernel(*refs, n_round):
    ya_ref, yb_ref, x_ref, w_ref, gna_ref, gnb_ref, g_ref, b_ref = refs[:8]
    out_ref = refs[8 + n_round]
    _round_slices(refs[8:8 + n_round], refs[9 + n_round:])
    ya = _rms_gain(ya_ref[...].astype(F32), gna_ref[...]).astype(BF16)
    yb = _rms_gain(yb_ref[...].astype(F32), gnb_ref[...]).astype(BF16)
    y = _dot(jnp.concatenate([ya, yb], axis=1), w_ref[...])
    out_ref[...] = _layernorm(ALPHA * x_ref[...] + y, g_ref[...], b_ref[...])


def _layer_vec(width, layer):
    return pl.BlockSpec((None, 1, width), lambda *_: (layer, 0, 0))


def _outproj(ya, yb, x, w, gna, gnb, g, b, layer, round_stacks=(), tm=512):
    m = x.shape[0]
    row = lambda i: (i, 0)
    round_in, round_out, round_shape = _round_job(
        round_stacks, [layer + 1] * len(round_stacks), m // tm)
    outs = pl.pallas_call(
        functools.partial(_outproj_kernel, n_round=len(round_stacks)),
        grid=(m // tm,),
        in_specs=[pl.BlockSpec((tm, WIDTH_A), row), pl.BlockSpec((tm, WIDTH_B), row),
                  pl.BlockSpec((tm, D_MODEL), row),
                  _resident_weight(w),
                  _layer_vec(WIDTH_A, layer), _layer_vec(WIDTH_B, layer),
                  _layer_vec(D_MODEL, layer), _layer_vec(D_MODEL, layer)] + round_in,
        out_specs=[pl.BlockSpec((tm, D_MODEL), row)] + round_out,
        out_shape=[jax.ShapeDtypeStruct((m, D_MODEL), F32)] + round_shape,
        compiler_params=_params("arbitrary"),
        name="outproj",
    )(ya, yb, x, w, gna, gnb, g, b, *round_stacks)
    return outs[0], outs[1:]


def _memkv_kernel(mem_ref, w_ref, kv_ref):
    kv_ref[...] = _dot(mem_ref[...].astype(BF16), w_ref[...].astype(BF16)).astype(BF16)


def _memkv(mem, w_mkv, tn=2048):
    rows = mem.shape[0]
    return pl.pallas_call(
        _memkv_kernel,
        grid=(DEPTH, 2 * D_MODEL // tn),
        in_specs=[pl.BlockSpec((rows, D_MODEL), lambda i, j: (0, 0)),
                  pl.BlockSpec((None, D_MODEL, tn), lambda i, j: (i, 0, j))],
        out_specs=pl.BlockSpec((None, rows, tn), lambda i, j: (i, 0, j)),
        out_shape=jax.ShapeDtypeStruct((DEPTH, rows, 2 * D_MODEL), BF16),
        compiler_params=_params("arbitrary", "arbitrary"),
        name="memkv",
    )(mem, w_mkv)


def _memattn_kernel(x_ref, wq_ref, k_ref, v_ref, wo_ref, g_ref, b_ref, out_ref):
    x = x_ref[...]
    q = (_dot(x.astype(BF16), wq_ref[...]) * MEM_HEAD_DIM ** -0.5).astype(BF16)
    heads = []
    for h in range(N_MEM_HEADS):
        cols = slice(h * MEM_HEAD_DIM, (h + 1) * MEM_HEAD_DIM)
        s = _dot_nt(q[:, cols], k_ref[:, cols])
        m = jnp.max(s, axis=-1, keepdims=True)
        p = jnp.exp(s - m)
        l = jnp.sum(p, axis=-1, keepdims=True)
        o = _dot(p.astype(BF16), v_ref[:, cols])
        heads.append((o * (1.0 / l)).astype(BF16))
    y = _dot(jnp.concatenate(heads, axis=1), wo_ref[...])
    out_ref[...] = _layernorm(ALPHA * x + y, g_ref[...], b_ref[...])


def _memattn(x, wq, kv, wo, g, b, batch, seq, layer, tm=512):
    n_mem = kv.shape[1] // batch
    per_batch = seq // tm
    row = lambda i: (i, 0)
    weight = _resident_weight(wq)
    kv_spec = lambda half: pl.BlockSpec((None, n_mem, D_MODEL),
                                        lambda i: (layer, i // per_batch, half))
    return pl.pallas_call(
        _memattn_kernel,
        grid=(batch * seq // tm,),
        in_specs=[pl.BlockSpec((tm, D_MODEL), row), weight, kv_spec(0), kv_spec(1), weight,
                  _layer_vec(D_MODEL, layer), _layer_vec(D_MODEL, layer)],
        out_specs=pl.BlockSpec((tm, D_MODEL), row),
        out_shape=jax.ShapeDtypeStruct((batch * seq, D_MODEL), F32),
        compiler_params=_params("arbitrary"),
        name="memattn",
    )(x, wq, kv, kv, wo, g, b)


def _mlp_kernel(x_ref, wu_ref, wd_ref, g_ref, b_ref, out_ref):
    j = pl.program_id(1)

    @pl.when(j == 0)
    def _():
        out_ref[...] = ALPHA * x_ref[...]

    h = jnp.maximum(_dot(x_ref[...].astype(BF16), wu_ref[...].astype(BF16)), 0.0)
    out_ref[...] += _dot((h * h).astype(BF16), wd_ref[...].astype(BF16))

    @pl.when(j == pl.num_programs(1) - 1)
    def _():
        out_ref[...] = _layernorm(out_ref[...], g_ref[...], b_ref[...])


def _mlp(x, wu, wd, g, b, layer, tm=1024, tf=512):
    m = x.shape[0]
    row = lambda i, j: (i, 0)
    return pl.pallas_call(
        _mlp_kernel,
        grid=(m // tm, D_FF // tf),
        in_specs=[pl.BlockSpec((tm, D_MODEL), row),
                  pl.BlockSpec((None, D_MODEL, tf), lambda i, j: (layer, 0, j)),
                  pl.BlockSpec((None, tf, D_MODEL), lambda i, j: (layer, j, 0)),
                  _layer_vec(D_MODEL, layer), _layer_vec(D_MODEL, layer)],
        out_specs=pl.BlockSpec((tm, D_MODEL), row),
        out_shape=jax.ShapeDtypeStruct((m, D_MODEL), F32),
        compiler_params=_params("arbitrary", "arbitrary"),
        name="mlp",
    )(x, wu, wd, g, b)


def _rope_lane_tables(positions, head_dim, rot_dim):
    half = rot_dim // 2
    inv_freq = ROPE_THETA ** (-jnp.arange(0, rot_dim, 2, dtype=F32) / rot_dim)
    ang = positions.astype(F32).reshape(-1, 1) * inv_freq
    cos, sin = jnp.cos(ang), jnp.sin(ang)
    rows = ang.shape[0]
    pad = jnp.zeros((rows, head_dim - rot_dim), F32)
    zero = jnp.zeros((rows, half), F32)
    reps = BLOCK // head_dim
    c = jnp.tile(jnp.concatenate([cos, cos, pad + 1.0], axis=1), (1, reps))
    s_up = jnp.tile(jnp.concatenate([zero, sin, pad], axis=1), (1, reps))
    s_dn = jnp.tile(jnp.concatenate([-sin, zero, pad], axis=1), (1, reps))
    return c, s_up, s_dn


def kernel(x, mem, positions, w_in, gn_a, gn_b, sinks, w_out, ln_mix_g, ln_mix_b, w_mq, w_mkv,
           w_mo, ln_mem_g, ln_mem_b, w_up, w_down, ln_ff_g, ln_ff_b):
    batch, seq, _ = x.shape
    tabs_a = _rope_lane_tables(positions, HEAD_DIM_A, ROT_DIM_A)
    tabs_b = _rope_lane_tables(positions, HEAD_DIM_B, ROT_DIM_B)
    (w_in_i,) = _round_layer((w_in,), 0)
    later = (w_out, w_mq, w_mo)
    gn_a, gn_b, ln_mix_g, ln_mix_b, ln_mem_g, ln_mem_b, ln_ff_g, ln_ff_b = (
        v.reshape(DEPTH, 1, -1)
        for v in (gn_a, gn_b, ln_mix_g, ln_mix_b, ln_mem_g, ln_mem_b, ln_ff_g, ln_ff_b))
    kv = _memkv(mem.reshape(-1, D_MODEL), w_mkv)
    h = x.reshape(batch * seq, D_MODEL)
    for i in range(DEPTH):
        more = i + 1 < DEPTH
        jobs = [(w_in, i + 1)] * more + [(w, 0) for w in later] * (i == 0)
        (qa, ka, va, qb, kvb), rounded = _inproj(
            h, w_in_i, tabs_a, tabs_b, [w for w, _ in jobs], [l for _, l in jobs])
        next_in = rounded[:1] if more else ()
        if i == 0:
            w_out_i, w_mq_i, w_mo_i = rounded[-3:]
        ya = _attn_a(qa, ka, va, batch, seq)
        yb = _attn_b(sinks[i], qb, kvb, batch, seq)
        h, next_rest = _outproj(ya, yb, h, w_out_i, gn_a, gn_b, ln_mix_g, ln_mix_b, i,
                                later if more else ())
        h = _memattn(h, w_mq_i, kv, w_mo_i, ln_mem_g, ln_mem_b, batch, seq, i)
        h = _mlp(h, w_up, w_down, ln_ff_g, ln_ff_b, i)
        if more:
            (w_in_i,), (w_out_i, w_mq_i, w_mo_i) = next_in, next_rest
    return h.reshape(batch, seq, D_MODEL)
```
